```python
import jax
import jax.numpy as jnp
from jax import lax
import numpy as np

D_MODEL = 2048
BATCH = 2
SEQ = 4096
DEPTH = 1

CHUNK = 64
MLSTM_HEADS = 8
MLSTM_WIDTH = D_MODEL
MLSTM_HEAD_DIM = MLSTM_WIDTH // MLSTM_HEADS
QK_CONV_WIDTH = 4
POOL_GROUPS = 4
POOL_WIDTH = D_MODEL // 2
POOL_GROUP_DIM = POOL_WIDTH // POOL_GROUPS
POOL_WINDOWS = (2, 4, 8, 16)
RMS_EPS = 1e-6
IN_SPLITS = (MLSTM_WIDTH, MLSTM_WIDTH, MLSTM_WIDTH, MLSTM_WIDTH, MLSTM_WIDTH,
             MLSTM_HEADS, MLSTM_HEADS, POOL_WIDTH, POOL_WIDTH, D_MODEL, D_MODEL)
IN_WIDTH = 5 * MLSTM_WIDTH + 2 * MLSTM_HEADS + 2 * POOL_WIDTH + 2 * D_MODEL

kernel_name = 'hybrid_mlstm_pool_gated_block'


def _rmsnorm(x, w):
    xf = x.astype(jnp.float32)
    y = xf * lax.rsqrt(jnp.mean(xf * xf, axis=-1, keepdims=True) + RMS_EPS)
    return (y * w.astype(jnp.float32)).astype(x.dtype)


def _partition(t):
    parts, start = [], 0
    for width in IN_SPLITS:
        parts.append(t[..., start:start + width])
        start += width
    return parts


def _causal_depthwise_conv(x, w, b):
    c = x.shape[-1]
    y = lax.conv_general_dilated(
        x, w[:, None, :].astype(x.dtype), window_strides=(1,),
        padding=[(w.shape[0] - 1, 0)], dimension_numbers=('NWC', 'WIO', 'NWC'),
        feature_group_count=c)
    return y + b.astype(x.dtype)


def _to_chunks(t, nc):
    b = t.shape[0]
    t = t.reshape((b, nc, CHUNK) + t.shape[2:])
    t = jnp.moveaxis(t, 1, 0)
    return jnp.swapaxes(t, 2, 3)


def _mlstm_chunkwise(q, k, v, ig, lf):
    bsz, s, nh, d = q.shape
    nc = s // CHUNK
    causal = jnp.tril(jnp.ones((CHUNK, CHUNK), dtype=bool))
    xs = (_to_chunks(q, nc), _to_chunks(k, nc), _to_chunks(v, nc),
          _to_chunks(ig, nc), _to_chunks(lf, nc))

    def step(carry, inp):
        c_st, n_st, m_st = carry
        qc, kc, vc, ic, fc = inp
        b = jnp.cumsum(fc, axis=-1)
        log_d = b[..., :, None] - b[..., None, :] + ic[..., None, :]
        log_d = jnp.where(causal, log_d, -jnp.inf)
        log_inter = b + m_st[..., None]
        m_row = jnp.maximum(jnp.max(log_d, axis=-1), log_inter)
        p = jnp.exp(log_d - m_row[..., None]) * jnp.einsum('bhld,bhsd->bhls', qc, kc)
        w_inter = jnp.exp(log_inter - m_row)
        num = (jnp.einsum('bhls,bhsd->bhld', p, vc)
               + w_inter[..., None] * jnp.einsum('bhlk,bhkv->bhlv', qc, c_st))
        den = jnp.sum(p, axis=-1) + w_inter * jnp.einsum('bhlk,bhk->bhl', qc, n_st)
        h = num / jnp.maximum(jnp.abs(den), jnp.exp(-m_row))[..., None]
        g = b[..., -1]
        log_w = g[..., None] - b + ic
        m_new = jnp.maximum(g + m_st, jnp.max(log_w, axis=-1))
        decay = jnp.exp(g + m_st - m_new)
        w = jnp.exp(log_w - m_new[..., None])
        c_new = decay[..., None, None] * c_st + jnp.einsum('bhs,bhsk,bhsv->bhkv', w, kc, vc)
        n_new = decay[..., None] * n_st + jnp.einsum('bhs,bhsk->bhk', w, kc)
        return (c_new, n_new, m_new), h

    init = (jnp.zeros((bsz, nh, d, d), jnp.float32),
            jnp.zeros((bsz, nh, d), jnp.float32),
            jnp.zeros((bsz, nh), jnp.float32))
    _, hs = lax.scan(step, init, xs)
    hs = jnp.moveaxis(jnp.swapaxes(hs, 2, 3), 0, 1)
    return hs.reshape(bsz, s, nh, d)


def _multiscale_pool(u):
    bsz, s, _ = u.shape
    uf = u.astype(jnp.float32).reshape(bsz, s, POOL_GROUPS, POOL_GROUP_DIM)
    cs = jnp.concatenate(
        [jnp.zeros((bsz, 1, POOL_GROUPS, POOL_GROUP_DIM), jnp.float32),
         jnp.cumsum(uf, axis=1)], axis=1)
    t = jnp.arange(1, s + 1)[:, None]
    win = jnp.array(POOL_WINDOWS, dtype=jnp.int32)[None, :]
    lo = jnp.maximum(t - win, 0)
    grp = jnp.arange(POOL_GROUPS)[None, :]
    window_sum = cs[:, 1:] - cs[:, lo, grp]
    count = (t - lo).astype(jnp.float32)
    return window_sum / count[..., None] - uf


def setup_inputs(seed: int = 0) -> dict:
    key = jax.random.key(seed)
    ks = jax.random.split(key, 16)
    f32 = jnp.float32
    nrm = lambda k, shape, scale: (jax.random.normal(k, shape, f32) * scale)
    x = jax.random.normal(ks[0], (BATCH, SEQ, D_MODEL), f32)
    norm_pre_w = 1.0 + nrm(ks[1], (DEPTH, D_MODEL), 0.05)
    w_in = nrm(ks[2], (DEPTH, D_MODEL, IN_WIDTH), D_MODEL ** -0.5)
    mlstm_i_bias = nrm(ks[3], (DEPTH, MLSTM_HEADS), 0.1)
    mlstm_f_bias = (jnp.linspace(3.0, 6.0, MLSTM_HEADS, dtype=f32)[None, :]
                    + nrm(ks[4], (DEPTH, MLSTM_HEADS), 0.1))
    qk_conv_w = nrm(ks[5], (DEPTH, QK_CONV_WIDTH, 2 * MLSTM_WIDTH), QK_CONV_WIDTH ** -0.5)
    qk_conv_b = nrm(ks[6], (DEPTH, 2 * MLSTM_WIDTH), 0.01)
    mlstm_norm_w = 1.0 + nrm(ks[7], (DEPTH, MLSTM_WIDTH), 0.05)
    pool_w = nrm(ks[8], (DEPTH, POOL_GROUPS, POOL_GROUP_DIM, POOL_GROUP_DIM), POOL_GROUP_DIM ** -0.5)
    pool_scale = 1.0 + nrm(ks[9], (DEPTH, POOL_WIDTH), 0.1)
    w_proj_mlstm = nrm(ks[10], (DEPTH, MLSTM_WIDTH, D_MODEL), MLSTM_WIDTH ** -0.5)
    w_proj_pool = nrm(ks[11], (DEPTH, POOL_WIDTH, D_MODEL), POOL_WIDTH ** -0.5)
    w_out = nrm(ks[12], (DEPTH, D_MODEL, D_MODEL), D_MODEL ** -0.5)
    norm_post_w = 1.0 + nrm(ks[13], (DEPTH, D_MODEL), 0.05)
    return {'x': x, 'norm_pre_w': norm_pre_w, 'w_in': w_in,
            'mlstm_i_bias': mlstm_i_bias, 'mlstm_f_bias': mlstm_f_bias,
            'qk_conv_w': qk_conv_w, 'qk_conv_b': qk_conv_b,
            'mlstm_norm_w': mlstm_norm_w, 'pool_w': pool_w, 'pool_scale': pool_scale,
            'w_proj_mlstm': w_proj_mlstm, 'w_proj_pool': w_proj_pool,
            'w_out': w_out, 'norm_post_w': norm_post_w}


def reference(x, norm_pre_w, w_in, mlstm_i_bias, mlstm_f_bias, qk_conv_w, qk_conv_b,
              mlstm_norm_w, pool_w, pool_scale, w_proj_mlstm, w_proj_pool, w_out,
              norm_post_w):
    bsz, s, _ = x.shape
    f32 = jnp.float32
    for l in range(DEPTH):
        h = _rmsnorm(x, norm_pre_w[l])
        proj = jnp.einsum('bsd,de->bse', h, w_in[l])
        (q, k, v, o_g, z_a, i_pre, f_pre, u_b, z_b, g_a, g_b) = _partition(proj)

        qk = jax.nn.silu(_causal_depthwise_conv(jnp.concatenate([q, k], axis=-1),
                                                qk_conv_w[l], qk_conv_b[l]))
        q = qk[..., :MLSTM_WIDTH].astype(f32).reshape(bsz, s, MLSTM_HEADS, MLSTM_HEAD_DIM)
        k = qk[..., MLSTM_WIDTH:].astype(f32).reshape(bsz, s, MLSTM_HEADS, MLSTM_HEAD_DIM)
        v = v.astype(f32).reshape(bsz, s, MLSTM_HEADS, MLSTM_HEAD_DIM)
        q = q * (MLSTM_HEAD_DIM ** -0.5)
        log_i = i_pre.astype(f32) + mlstm_i_bias[l].astype(f32)
        log_f = jax.nn.log_sigmoid(f_pre.astype(f32) + mlstm_f_bias[l].astype(f32))
        h_t = _mlstm_chunkwise(q, k, v, log_i, log_f)
        h_t = h_t * lax.rsqrt(jnp.mean(h_t * h_t, axis=-1, keepdims=True) + RMS_EPS)
        h_t = h_t * mlstm_norm_w[l].astype(f32).reshape(MLSTM_HEADS, MLSTM_HEAD_DIM)
        h_a = jax.nn.sigmoid(o_g.astype(f32)) * h_t.reshape(bsz, s, MLSTM_WIDTH)
        y_a = (h_a * jax.nn.silu(z_a.astype(f32))).astype(x.dtype)

        pooled = _multiscale_pool(u_b)
        mixed = jnp.einsum('bsgc,gcd->bsgd', pooled, pool_w[l].astype(f32))
        mixed = mixed.reshape(bsz, s, POOL_WIDTH) * pool_scale[l].astype(f32)
        y_b = (mixed * jax.nn.silu(z_b.astype(f32))).astype(x.dtype)

        br_a = jnp.einsum('bsc,cd->bsd', y_a, w_proj_mlstm[l])
        br_b = jnp.einsum('bsc,cd->bsd', y_b, w_proj_pool[l])
        merged = jax.nn.sigmoid(g_a) * br_a + jax.nn.sigmoid(g_b) * br_b
        out = jnp.einsum('bsd,de->bse', merged, w_out[l])
        x = x + _rmsnorm(out, norm_post_w[l])
    return x
```

```python
import functools

import jax
import jax.numpy as jnp
from jax import lax
from jax.experimental import pallas as pl
from jax.experimental.pallas import tpu as pltpu

D_MODEL = 2048
HEADS = 8
HEAD_DIM = D_MODEL // HEADS
CONV_WIDTH = 4
POOL_GROUPS = 4
POOL_WIDTH = D_MODEL // 2
POOL_GROUP_DIM = POOL_WIDTH // POOL_GROUPS
POOL_WINDOWS = (2, 4, 8, 16)
RMS_EPS = 1e-6
GATE_COLS = 2 * HEADS
OFF_GATES = 5 * D_MODEL
OFF_POOL = OFF_GATES + GATE_COLS
MAIN_COLS = 5 * D_MODEL + 2 * POOL_WIDTH + 2 * D_MODEL

SUBLANES = 8
HALO_CONV = SUBLANES
HALO_POOL = 2 * SUBLANES
VMEM_LIMIT = 48 * 1024 * 1024

F32 = jnp.float32
BF16 = jnp.bfloat16


def _silu(x):
    return x * jax.nn.sigmoid(x)


def _log_sigmoid(x):
    return jnp.minimum(x, 0.0) - jnp.log1p(jnp.exp(-jnp.abs(x)))


def _inproj_kernel(x_ref, nw_ref, w_ref, wg_ref, wgt_ref, proj_ref, gcol_ref, grow_ref, h_ref):
    @pl.when(pl.program_id(1) == 0)
    def _():
        x = x_ref[...]
        y = x * lax.rsqrt(jnp.mean(x * x, axis=-1, keepdims=True) + RMS_EPS) * nw_ref[...]
        hb = y.astype(BF16)
        h_ref[...] = hb
        gcol_ref[...] = jnp.dot(hb, wg_ref[...], preferred_element_type=F32)
        grow_ref[...] = lax.dot_general(wgt_ref[...], hb, (((1,), (1,)), ((), ())),
                                        preferred_element_type=F32)

    proj_ref[...] = jnp.dot(h_ref[...], w_ref[...], preferred_element_type=F32).astype(BF16)


def _inproj(x2, norm_w, w_main, w_gate, w_gate_t, tm, tn):
    t = x2.shape[0]
    n = w_main.shape[1]
    return pl.pallas_call(
        _inproj_kernel,
        grid=(t // tm, n // tn),
        in_specs=[
            pl.BlockSpec((tm, D_MODEL), lambda i, j: (i, 0)),
            pl.BlockSpec((1, D_MODEL), lambda i, j: (0, 0)),
            pl.BlockSpec((D_MODEL, tn), lambda i, j: (0, j)),
            pl.BlockSpec((D_MODEL, GATE_COLS), lambda i, j: (0, 0)),
            pl.BlockSpec((GATE_COLS, D_MODEL), lambda i, j: (0, 0)),
        ],
        out_specs=[
            pl.BlockSpec((tm, tn), lambda i, j: (i, j)),
            pl.BlockSpec((tm, GATE_COLS), lambda i, j: (i, 0)),
            pl.BlockSpec((GATE_COLS, tm), lambda i, j: (0, i)),
        ],
        out_shape=[
            jax.ShapeDtypeStruct((t, n), BF16),
            jax.ShapeDtypeStruct((t, GATE_COLS), F32),
            jax.ShapeDtypeStruct((GATE_COLS, t), F32),
        ],
        scratch_shapes=[pltpu.VMEM((tm, D_MODEL), BF16)],
        compiler_params=pltpu.CompilerParams(
            dimension_semantics=("parallel", "arbitrary"), vmem_limit_bytes=VMEM_LIMIT),
        name="inproj",
    )(x2, norm_w, w_main, w_gate, w_gate_t)


def _mlstm_kernel(q_ref, k_ref, v_ref, o_ref, z_ref, gcol_ref, grow_ref, brow_ref, bcol_ref,
                  cw_ref, cb_ref, nw_ref, y_ref, ext_ref, c_ref, n_ref, m_ref, *, chunk):
    L = chunk
    first = pl.program_id(1) == 0

    @pl.when(first)
    def _():
        ext_ref[0:HALO_CONV, :] = jnp.zeros((HALO_CONV, 2 * D_MODEL), F32)
        c_ref[...] = jnp.zeros_like(c_ref)
        n_ref[...] = jnp.zeros_like(n_ref)
        m_ref[...] = jnp.zeros_like(m_ref)

    ext_ref[HALO_CONV:HALO_CONV + L, 0:D_MODEL] = q_ref[...].astype(F32)
    ext_ref[HALO_CONV:HALO_CONV + L, D_MODEL:2 * D_MODEL] = k_ref[...].astype(F32)

    g_col = gcol_ref[...] + brow_ref[...]
    g_row = grow_ref[...] + bcol_ref[...]
    lf_col = _log_sigmoid(g_col[:, HEADS:GATE_COLS])
    lf_row = _log_sigmoid(g_row[HEADS:GATE_COLS, :])

    t_idx = lax.broadcasted_iota(jnp.int32, (L, L), 0)
    s_idx = lax.broadcasted_iota(jnp.int32, (L, L), 1)
    causal = s_idx <= t_idx
    scale = HEAD_DIM ** -0.5

    def conv(base, col):
        acc = cb_ref[:, base + col:base + col + HEAD_DIM]
        for j in range(CONV_WIDTH):
            r0 = HALO_CONV - (CONV_WIDTH - 1) + j
            acc = acc + (cw_ref[j:j + 1, base + col:base + col + HEAD_DIM]
                         * ext_ref[r0:r0 + L, base + col:base + col + HEAD_DIM])
        return _silu(acc)

    for hh in range(HEADS):
        col = hh * HEAD_DIM
        q = conv(0, col) * scale
        k = conv(D_MODEL, col)
        qb = q.astype(BF16)
        kb = k.astype(BF16)
        vb = v_ref[:, col:col + HEAD_DIM]

        i_row = g_row[hh:hh + 1, :]
        i_col = g_col[:, hh:hh + 1]
        f_row = lf_row[hh:hh + 1, :]
        f_col = lf_col[:, hh:hh + 1]
        b_col = jnp.sum(jnp.where(causal, f_row, 0.0), axis=1, keepdims=True)
        b_row = jnp.sum(jnp.where(t_idx <= s_idx, f_col, 0.0), axis=0, keepdims=True)

        m_st = m_ref[hh]
        c_st = c_ref[hh]
        n_st = n_ref[hh]

        log_d = jnp.where(causal, b_col - b_row + i_row, -jnp.inf)
        log_inter = b_col + m_st
        m_row = jnp.maximum(jnp.max(log_d, axis=1, keepdims=True), log_inter)
        s = lax.dot_general(qb, kb, (((1,), (1,)), ((), ())), preferred_element_type=F32)
        p = jnp.exp(log_d - m_row) * s
        w_inter = jnp.exp(log_inter - m_row)
        num = (jnp.dot(p.astype(BF16), vb, preferred_element_type=F32)
               + w_inter * jnp.dot(qb, c_st.astype(BF16), preferred_element_type=F32))
        den = (jnp.sum(p, axis=1, keepdims=True)
               + w_inter * jnp.sum(q * n_st, axis=1, keepdims=True))
        h = num / jnp.maximum(jnp.abs(den), jnp.exp(-m_row))

        g = b_col[L - 1:L, :]
        log_w = g - b_col + i_col
        m_new = jnp.maximum(g + m_st, jnp.max(log_w, axis=0, keepdims=True))
        decay = jnp.exp(g + m_st - m_new)
        wk = jnp.exp(log_w - m_new) * k
        c_ref[hh] = decay * c_st + lax.dot_general(
            wk.astype(BF16), vb, (((0,), (0,)), ((), ())), preferred_element_type=F32)
        n_ref[hh] = decay * n_st + jnp.sum(wk, axis=0, keepdims=True)
        m_ref[hh] = m_new

        hn = h * lax.rsqrt(jnp.mean(h * h, axis=1, keepdims=True) + RMS_EPS)
        hn = hn * nw_ref[:, col:col + HEAD_DIM]
        og = jax.nn.sigmoid(o_ref[:, col:col + HEAD_DIM].astype(F32))
        za = _silu(z_ref[:, col:col + HEAD_DIM].astype(F32))
        y_ref[:, col:col + HEAD_DIM] = (og * hn * za).astype(BF16)

    ext_ref[0:HALO_CONV, :] = ext_ref[L:L + HALO_CONV, :]


def _mlstm(proj, gcol, grow, bias_row, bias_col, conv_w, conv_b, norm_w, bsz, seq, chunk):
    nc = seq // chunk
    t = bsz * seq
    col_block = lambda cb: pl.BlockSpec((chunk, D_MODEL), lambda b, c: (b * nc + c, cb))
    const = lambda shape: pl.BlockSpec(shape, lambda b, c: (0,) * len(shape))
    return pl.pallas_call(
        functools.partial(_mlstm_kernel, chunk=chunk),
        grid=(bsz, nc),
        in_specs=[
            col_block(0), col_block(1), col_block(2), col_block(3), col_block(4),
            pl.BlockSpec((chunk, GATE_COLS), lambda b, c: (b * nc + c, 0)),
            pl.BlockSpec((GATE_COLS, chunk), lambda b, c: (0, b * nc + c)),
            const((1, GATE_COLS)), const((GATE_COLS, 1)),
            const((CONV_WIDTH, 2 * D_MODEL)), const((1, 2 * D_MODEL)), const((1, D_MODEL)),
        ],
        out_specs=pl.BlockSpec((chunk, D_MODEL), lambda b, c: (b * nc + c, 0)),
        out_shape=jax.ShapeDtypeStruct((t, D_MODEL), BF16),
        scratch_shapes=[
            pltpu.VMEM((chunk + HALO_CONV, 2 * D_MODEL), F32),
            pltpu.VMEM((HEADS, HEAD_DIM, HEAD_DIM), F32),
            pltpu.VMEM((HEADS, 1, HEAD_DIM), F32),
            pltpu.VMEM((HEADS, 1, 1), F32),
        ],
        compiler_params=pltpu.CompilerParams(
            dimension_semantics=("parallel", "arbitrary"), vmem_limit_bytes=VMEM_LIMIT),
        name="mlstm",
    )(proj, proj, proj, proj, proj, gcol, grow, bias_row, bias_col, conv_w, conv_b, norm_w)


def _pool_kernel(u_ref, z_ref, pw_ref, ps_ref, y_ref, ext_ref, *, rows):
    c = pl.program_id(1)

    @pl.when(c == 0)
    def _():
        ext_ref[0:HALO_POOL, :] = jnp.zeros((HALO_POOL, POOL_WIDTH), F32)

    ext_ref[HALO_POOL:HALO_POOL + rows, :] = u_ref[...].astype(F32)
    pos = lax.broadcasted_iota(jnp.int32, (rows, 1), 0) + c * rows + 1

    for g in range(POOL_GROUPS):
        col = g * POOL_GROUP_DIM
        win = POOL_WINDOWS[g]
        u = ext_ref[HALO_POOL:HALO_POOL + rows, col:col + POOL_GROUP_DIM]
        wsum = u
        for j in range(1, win):
            wsum = wsum + ext_ref[HALO_POOL - j:HALO_POOL - j + rows, col:col + POOL_GROUP_DIM]
        count = jnp.minimum(pos, win).astype(F32)
        pooled = wsum / count - u
        mixed = jnp.dot(pooled.astype(BF16), pw_ref[g], preferred_element_type=F32)
        mixed = mixed * ps_ref[:, col:col + POOL_GROUP_DIM]
        zb = _silu(z_ref[:, col:col + POOL_GROUP_DIM].astype(F32))
        y_ref[:, col:col + POOL_GROUP_DIM] = (mixed * zb).astype(BF16)

    ext_ref[0:HALO_POOL, :] = ext_ref[rows:rows + HALO_POOL, :]


def _pool(proj, pool_w, pool_scale, bsz, seq, rows):
    nc = seq // rows
    t = bsz * seq
    u_block = OFF_GATES // POOL_WIDTH
    return pl.pallas_call(
        functools.partial(_pool_kernel, rows=rows),
        grid=(bsz, nc),
        in_specs=[
            pl.BlockSpec((rows, POOL_WIDTH), lambda b, c: (b * nc + c, u_block)),
            pl.BlockSpec((rows, POOL_WIDTH), lambda b, c: (b * nc + c, u_block + 1)),
            pl.BlockSpec((POOL_GROUPS, POOL_GROUP_DIM, POOL_GROUP_DIM), lambda b, c: (0, 0, 0)),
            pl.BlockSpec((1, POOL_WIDTH), lambda b, c: (0, 0)),
        ],
        out_specs=pl.BlockSpec((rows, POOL_WIDTH), lambda b, c: (b * nc + c, 0)),
        out_shape=jax.ShapeDtypeStruct((t, POOL_WIDTH), BF16),
        scratch_shapes=[pltpu.VMEM((rows + HALO_POOL, POOL_WIDTH), F32)],
        compiler_params=pltpu.CompilerParams(
            dimension_semantics=("parallel", "arbitrary"), vmem_limit_bytes=VMEM_LIMIT),
        name="pool",
    )(proj, proj, pool_w, pool_scale)


def _merge_kernel(ya_ref, yb_ref, ga_ref, gb_ref, wa_ref, wb_ref, o_ref):
    br_a = jnp.dot(ya_ref[...], wa_ref[...], preferred_element_type=F32)
    br_b = jnp.dot(yb_ref[...], wb_ref[...], preferred_element_type=F32)
    merged = (jax.nn.sigmoid(ga_ref[...].astype(F32)) * br_a
              + jax.nn.sigmoid(gb_ref[...].astype(F32)) * br_b)
    o_ref[...] = merged.astype(BF16)


def _merge(y_a, y_b, proj, w_a, w_b, tm):
    t = y_a.shape[0]
    ga_block = (OFF_GATES + 2 * POOL_WIDTH) // D_MODEL
    return pl.pallas_call(
        _merge_kernel,
        grid=(t // tm,),
        in_specs=[
            pl.BlockSpec((tm, D_MODEL), lambda i: (i, 0)),
            pl.BlockSpec((tm, POOL_WIDTH), lambda i: (i, 0)),
            pl.BlockSpec((tm, D_MODEL), lambda i: (i, ga_block)),
            pl.BlockSpec((tm, D_MODEL), lambda i: (i, ga_block + 1)),
            pl.BlockSpec((D_MODEL, D_MODEL), lambda i: (0, 0)),
            pl.BlockSpec((POOL_WIDTH, D_MODEL), lambda i: (0, 0)),
        ],
        out_specs=pl.BlockSpec((tm, D_MODEL), lambda i: (i, 0)),
        out_shape=jax.ShapeDtypeStruct((t, D_MODEL), BF16),
        compiler_params=pltpu.CompilerParams(
            dimension_semantics=("parallel",), vmem_limit_bytes=VMEM_LIMIT),
        name="merge",
    )(y_a, y_b, proj, proj, w_a, w_b)


def _outproj_kernel(m_ref, w_ref, x_ref, nw_ref, o_ref):
    out = jnp.dot(m_ref[...], w_ref[...], preferred_element_type=F32)
    y = out * lax.rsqrt(jnp.mean(out * out, axis=-1, keepdims=True) + RMS_EPS) * nw_ref[...]
    o_ref[...] = x_ref[...] + y


def _outproj(merged, w_out, x2, norm_w, tm):
    t = merged.shape[0]
    return pl.pallas_call(
        _outproj_kernel,
        grid=(t // tm,),
        in_specs=[
            pl.BlockSpec((tm, D_MODEL), lambda i: (i, 0)),
            pl.BlockSpec((D_MODEL, D_MODEL), lambda i: (0, 0)),
            pl.BlockSpec((tm, D_MODEL), lambda i: (i, 0)),
            pl.BlockSpec((1, D_MODEL), lambda i: (0, 0)),
        ],
        out_specs=pl.BlockSpec((tm, D_MODEL), lambda i: (i, 0)),
        out_shape=jax.ShapeDtypeStruct((t, D_MODEL), F32),
        compiler_params=pltpu.CompilerParams(
            dimension_semantics=("parallel",), vmem_limit_bytes=VMEM_LIMIT),
        name="outproj",
    )(merged, w_out, x2, norm_w)


def _layer(x2, bsz, seq, norm_pre_w, w_in, i_bias, f_bias, conv_w, conv_b, mlstm_norm_w,
           pool_w, pool_scale, w_proj_mlstm, w_proj_pool, w_out, norm_post_w):
    w_main = jnp.concatenate([w_in[:, :OFF_GATES], w_in[:, OFF_POOL:]], axis=1).astype(BF16)
    w_gate = w_in[:, OFF_GATES:OFF_POOL].astype(BF16)
    bias = jnp.concatenate([i_bias, f_bias]).astype(F32)

    proj, gcol, grow = _inproj(x2, norm_pre_w[None, :], w_main, w_gate, w_gate.T, tm=1024, tn=1024)
    y_a = _mlstm(proj, gcol, grow, bias[None, :], bias[:, None], conv_w, conv_b[None, :],
                 mlstm_norm_w[None, :], bsz, seq, chunk=256)
    y_b = _pool(proj, pool_w.astype(BF16), pool_scale[None, :], bsz, seq, rows=512)
    merged = _merge(y_a, y_b, proj, w_proj_mlstm.astype(BF16), w_proj_pool.astype(BF16), tm=512)
    return _outproj(merged, w_out.astype(BF16), x2, norm_post_w[None, :], tm=512)


def kernel(x, norm_pre_w, w_in, mlstm_i_bias, mlstm_f_bias, qk_conv_w, qk_conv_b, mlstm_norm_w,
           pool_w, pool_scale, w_proj_mlstm, w_proj_pool, w_out, norm_post_w):
    bsz, seq, d = x.shape
    assert d == D_MODEL and w_in.shape[-1] == MAIN_COLS + GATE_COLS
    x2 = x.reshape(bsz * seq, d)
    for l in range(norm_pre_w.shape[0]):
        x2 = _layer(x2, bsz, seq, norm_pre_w[l], w_in[l], mlstm_i_bias[l], mlstm_f_bias[l],
                    qk_conv_w[l], qk_conv_b[l], mlstm_norm_w[l], pool_w[l], pool_scale[l],
                    w_proj_mlstm[l], w_proj_pool[l], w_out[l], norm_post_w[l])
    return x2.reshape(bsz, seq, d)
```

```python
import functools

import jax
import jax.numpy as jnp
from jax import lax
from jax.experimental import pallas as pl
from jax.experimental.pallas import tpu as pltpu

D_MODEL = 2048
HEADS = 8
HEAD_DIM = D_MODEL // HEADS
CONV_WIDTH = 4
POOL_GROUPS = 4
POOL_WIDTH = D_MODEL // 2
POOL_GROUP_DIM = POOL_WIDTH // POOL_GROUPS
POOL_WINDOWS = (2, 4, 8, 16)
RMS_EPS = 1e-6
GATE_COLS = 2 * HEADS
OFF_GATES = 5 * D_MODEL
OFF_POOL = OFF_GATES + GATE_COLS
MAIN_COLS = 5 * D_MODEL + 2 * POOL_WIDTH + 2 * D_MODEL

SUBLANES = 8
LANES = 128
MXU_COLS = 256
HALO_POOL = 2 * SUBLANES
VMEM_LIMIT = 56 * 1024 * 1024

F32 = jnp.float32
BF16 = jnp.bfloat16


def _silu(x):
    return x * jax.nn.sigmoid(x)


def _log_sigmoid(x):
    return jnp.minimum(x, 0.0) - jnp.log1p(jnp.exp(-jnp.abs(x)))


def _prenorm_kernel(x_ref, nw_ref, wg_ref, wgt_ref, h_ref, gcol_ref, grow_ref):
    x = x_ref[...]
    y = x * lax.rsqrt(jnp.mean(x * x, axis=-1, keepdims=True) + RMS_EPS) * nw_ref[...]
    hb = y.astype(BF16)
    h_ref[...] = hb
    gcol_ref[...] = jnp.dot(hb, wg_ref[...], preferred_element_type=F32)
    grow_ref[...] = lax.dot_general(wgt_ref[...], hb, (((1,), (1,)), ((), ())),
                                    preferred_element_type=F32)


def _prenorm(x2, norm_w, w_gate, w_gate_t, tm):
    t = x2.shape[0]
    return pl.pallas_call(
        _prenorm_kernel,
        grid=(t // tm,),
        in_specs=[
            pl.BlockSpec((tm, D_MODEL), lambda i: (i, 0)),
            pl.BlockSpec((1, D_MODEL), lambda i: (0, 0)),
            pl.BlockSpec((D_MODEL, GATE_COLS), lambda i: (0, 0)),
            pl.BlockSpec((GATE_COLS, D_MODEL), lambda i: (0, 0)),
        ],
        out_specs=[
            pl.BlockSpec((tm, D_MODEL), lambda i: (i, 0)),
            pl.BlockSpec((tm, GATE_COLS), lambda i: (i, 0)),
            pl.BlockSpec((GATE_COLS, tm), lambda i: (0, i)),
        ],
        out_shape=[
            jax.ShapeDtypeStruct((t, D_MODEL), BF16),
            jax.ShapeDtypeStruct((t, GATE_COLS), F32),
            jax.ShapeDtypeStruct((GATE_COLS, t), F32),
        ],
        compiler_params=pltpu.CompilerParams(
            dimension_semantics=("parallel",), vmem_limit_bytes=VMEM_LIMIT),
        name="prenorm",
    )(x2, norm_w, w_gate, w_gate_t)


def _proj_kernel(h_ref, w_ref, wn_ref, cw_ref, cb_ref, o_ref, wb_ref, halo_ref,
                 *, tm, tn, tiles_per_seq):
    j = pl.program_id(0)
    i = pl.program_id(1)
    seg = D_MODEL // tn
    shift_tile = OFF_GATES // tn
    krows = 256

    @pl.when(jnp.logical_and(i == 0, j < shift_tile))
    def _():
        for r in range(0, D_MODEL, krows):
            wb_ref[r:r + krows, :] = w_ref[r:r + krows, :].astype(BF16)

    @pl.when(jnp.logical_and(i == 0, j >= shift_tile))
    def _():
        for r in range(0, D_MODEL, krows):
            cat = jnp.concatenate([w_ref[r:r + krows, :], wn_ref[r:r + krows, :]], axis=1)
            shifted = pltpu.roll(cat, tn + LANES - GATE_COLS, 1)
            wb_ref[r:r + krows, :] = shifted[:, :tn].astype(BF16)

    def stripes(epilogue):
        for c in range(0, tn, MXU_COLS):
            r = jnp.dot(h_ref[...], wb_ref[:, c:c + MXU_COLS], preferred_element_type=F32)
            o_ref[:, c:c + MXU_COLS] = epilogue(r, c).astype(BF16)

    is_conv = j < 2 * seg
    pool_tile = shift_tile
    is_plain = jnp.logical_or(jnp.logical_and(j >= 2 * seg, j < 3 * seg), j == pool_tile)
    is_silu = jnp.logical_or(jnp.logical_and(j >= 4 * seg, j < 5 * seg),
                             jnp.logical_and(j > pool_tile, j < pool_tile + 2 * (POOL_WIDTH // tn)))
    is_sigmoid = jnp.logical_not(jnp.logical_or(jnp.logical_or(is_conv, is_plain), is_silu))

    @pl.when(is_conv)
    def _():
        @pl.when(i % tiles_per_seq == 0)
        def _():
            halo_ref[...] = jnp.zeros_like(halo_ref)

        out_scale = jnp.where(j < seg, HEAD_DIM ** -0.5, 1.0).astype(F32)

        def epilogue(r, c):
            ext = jnp.concatenate([halo_ref[:, c:c + MXU_COLS], r], axis=0)
            acc = cw_ref[0:1, c:c + MXU_COLS] * ext
            for k in range(1, CONV_WIDTH):
                acc = cw_ref[k:k + 1, c:c + MXU_COLS] * ext + pltpu.roll(acc, 1, 0)
            halo_ref[:, c:c + MXU_COLS] = r[tm - SUBLANES:, :]
            y = acc[SUBLANES:, :] + cb_ref[:, c:c + MXU_COLS]
            return _silu(y) * out_scale

        stripes(epilogue)

    @pl.when(is_plain)
    def _():
        stripes(lambda r, c: r)

    @pl.when(is_silu)
    def _():
        stripes(lambda r, c: _silu(r))

    @pl.when(is_sigmoid)
    def _():
        stripes(lambda r, c: jax.nn.sigmoid(r))


def _proj(h, w_in, conv_w, conv_b, seq, tm, tn):
    t = h.shape[0]
    n_tiles = MAIN_COLS // tn
    shift_tile = OFF_GATES // tn
    conv_tiles = 2 * D_MODEL // tn
    lanes_per_tile = tn // LANES
    conv_block = lambda j, i: (0, jnp.minimum(j, conv_tiles - 1))
    return pl.pallas_call(
        functools.partial(_proj_kernel, tm=tm, tn=tn, tiles_per_seq=seq // tm),
        grid=(n_tiles, t // tm),
        in_specs=[
            pl.BlockSpec((tm, D_MODEL), lambda j, i: (i, 0)),
            pl.BlockSpec((D_MODEL, tn), lambda j, i: (0, j)),
            pl.BlockSpec((D_MODEL, LANES),
                         lambda j, i: (0, jnp.where(j >= shift_tile, (j + 1) * lanes_per_tile, 0))),
            pl.BlockSpec((CONV_WIDTH, tn), conv_block),
            pl.BlockSpec((1, tn), conv_block),
        ],
        out_specs=pl.BlockSpec((tm, tn), lambda j, i: (i, j)),
        out_shape=jax.ShapeDtypeStruct((t, MAIN_COLS), BF16),
        scratch_shapes=[pltpu.VMEM((D_MODEL, tn), BF16), pltpu.VMEM((SUBLANES, tn), F32)],
        compiler_params=pltpu.CompilerParams(
            dimension_semantics=("arbitrary", "arbitrary"), vmem_limit_bytes=VMEM_LIMIT),
        name="proj",
    )(h, w_in, w_in, conv_w, conv_b)


def _mlstm_kernel(q_ref, k_ref, v_ref, o_ref, z_ref, gcol_ref, grow_ref, brow_ref, bcol_ref,
                  nw_ref, y_ref, c_ref, n_ref, m_ref, *, chunk):
    L = chunk

    @pl.when(pl.program_id(1) == 0)
    def _():
        c_ref[...] = jnp.zeros_like(c_ref)
        n_ref[...] = jnp.zeros_like(n_ref)
        m_ref[...] = jnp.zeros_like(m_ref)

    g_col = gcol_ref[...] + brow_ref[...]
    g_row = grow_ref[...] + bcol_ref[...]
    lf_col = _log_sigmoid(g_col[:, HEADS:GATE_COLS])
    lf_row = _log_sigmoid(g_row[HEADS:GATE_COLS, :])

    t_idx = lax.broadcasted_iota(jnp.int32, (L, L), 0)
    s_idx = lax.broadcasted_iota(jnp.int32, (L, L), 1)
    causal = s_idx <= t_idx

    for hh in range(HEADS):
        col = hh * HEAD_DIM
        qb = q_ref[:, col:col + HEAD_DIM]
        kb = k_ref[:, col:col + HEAD_DIM]
        vb = v_ref[:, col:col + HEAD_DIM]

        i_row = g_row[hh:hh + 1, :]
        i_col = g_col[:, hh:hh + 1]
        f_row = lf_row[hh:hh + 1, :]
        f_col = lf_col[:, hh:hh + 1]
        b_col = jnp.sum(jnp.where(causal, f_row, 0.0), axis=1, keepdims=True)
        b_row = jnp.sum(jnp.where(t_idx <= s_idx, f_col, 0.0), axis=0, keepdims=True)

        m_st = m_ref[hh]
        c_st = c_ref[hh]
        n_st = n_ref[hh]

        log_d = jnp.where(causal, b_col - b_row + i_row, -jnp.inf)
        log_inter = b_col + m_st
        m_row = jnp.maximum(jnp.max(log_d, axis=1, keepdims=True), log_inter)
        s = lax.dot_general(qb, kb, (((1,), (1,)), ((), ())), preferred_element_type=F32)
        p = jnp.exp(log_d - m_row) * s
        w_inter = jnp.exp(log_inter - m_row)
        num = (jnp.dot(p.astype(BF16), vb, preferred_element_type=F32)
               + w_inter * jnp.dot(qb, c_st.astype(BF16), preferred_element_type=F32))
        q_n = jnp.sum(qb.astype(F32) * n_st, axis=1, keepdims=True)
        den = jnp.sum(p, axis=1, keepdims=True) + w_inter * q_n
        h = num / jnp.maximum(jnp.abs(den), jnp.exp(-m_row))

        g = b_col[L - 1:L, :]
        log_w = g - b_col + i_col
        m_new = jnp.maximum(g + m_st, jnp.max(log_w, axis=0, keepdims=True))
        decay = jnp.exp(g + m_st - m_new)
        wk = jnp.exp(log_w - m_new) * kb.astype(F32)
        c_ref[hh] = decay * c_st + lax.dot_general(
            wk.astype(BF16), vb, (((0,), (0,)), ((), ())), preferred_element_type=F32)
        n_ref[hh] = decay * n_st + jnp.sum(wk, axis=0, keepdims=True)
        m_ref[hh] = m_new

        hn = h * lax.rsqrt(jnp.mean(h * h, axis=1, keepdims=True) + RMS_EPS)
        hn = hn * nw_ref[:, col:col + HEAD_DIM]
        og = o_ref[:, col:col + HEAD_DIM].astype(F32)
        za = z_ref[:, col:col + HEAD_DIM].astype(F32)
        y_ref[:, col:col + HEAD_DIM] = (og * hn * za).astype(BF16)


def _mlstm(proj, gcol, grow, bias_row, bias_col, norm_w, bsz, seq, chunk):
    nc = seq // chunk
    t = bsz * seq
    col_block = lambda cb: pl.BlockSpec((chunk, D_MODEL), lambda b, c: (b * nc + c, cb))
    const = lambda shape: pl.BlockSpec(shape, lambda b, c: (0,) * len(shape))
    return pl.pallas_call(
        functools.partial(_mlstm_kernel, chunk=chunk),
        grid=(bsz, nc),
        in_specs=[
            col_block(0), col_block(1), col_block(2), col_block(3), col_block(4),
            pl.BlockSpec((chunk, GATE_COLS), lambda b, c: (b * nc + c, 0)),
            pl.BlockSpec((GATE_COLS, chunk), lambda b, c: (0, b * nc + c)),
            const((1, GATE_COLS)), const((GATE_COLS, 1)), const((1, D_MODEL)),
        ],
        out_specs=pl.BlockSpec((chunk, D_MODEL), lambda b, c: (b * nc + c, 0)),
        out_shape=jax.ShapeDtypeStruct((t, D_MODEL), BF16),
        scratch_shapes=[
            pltpu.VMEM((HEADS, HEAD_DIM, HEAD_DIM), F32),
            pltpu.VMEM((HEADS, 1, HEAD_DIM), F32),
            pltpu.VMEM((HEADS, 1, 1), F32),
        ],
        compiler_params=pltpu.CompilerParams(
            dimension_semantics=("parallel", "arbitrary"), vmem_limit_bytes=VMEM_LIMIT),
        name="mlstm",
    )(proj, proj, proj, proj, proj, gcol, grow, bias_row, bias_col, norm_w)


def _pool_kernel(u_ref, z_ref, pw_ref, ps_ref, y_ref, ext_ref, *, rows):
    c = pl.program_id(1)

    @pl.when(c == 0)
    def _():
        ext_ref[0:HALO_POOL, :] = jnp.zeros((HALO_POOL, POOL_WIDTH), F32)

    ext_ref[HALO_POOL:HALO_POOL + rows, :] = u_ref[...].astype(F32)
    pos = lax.broadcasted_iota(jnp.int32, (rows, 1), 0) + c * rows + 1

    for g in range(POOL_GROUPS):
        col = g * POOL_GROUP_DIM
        win = POOL_WINDOWS[g]
        u = ext_ref[HALO_POOL:HALO_POOL + rows, col:col + POOL_GROUP_DIM]
        wsum = u
        for j in range(1, win):
            wsum = wsum + ext_ref[HALO_POOL - j:HALO_POOL - j + rows, col:col + POOL_GROUP_DIM]
        count = jnp.minimum(pos, win).astype(F32)
        pooled = wsum / count - u
        mixed = jnp.dot(pooled.astype(BF16), pw_ref[g], preferred_element_type=F32)
        mixed = mixed * ps_ref[:, col:col + POOL_GROUP_DIM]
        zb = z_ref[:, col:col + POOL_GROUP_DIM].astype(F32)
        y_ref[:, col:col + POOL_GROUP_DIM] = (mixed * zb).astype(BF16)

    ext_ref[0:HALO_POOL, :] = ext_ref[rows:rows + HALO_POOL, :]


def _pool(proj, pool_w, pool_scale, bsz, seq, rows):
    nc = seq // rows
    t = bsz * seq
    u_block = OFF_GATES // POOL_WIDTH
    return pl.pallas_call(
        functools.partial(_pool_kernel, rows=rows),
        grid=(bsz, nc),
        in_specs=[
            pl.BlockSpec((rows, POOL_WIDTH), lambda b, c: (b * nc + c, u_block)),
            pl.BlockSpec((rows, POOL_WIDTH), lambda b, c: (b * nc + c, u_block + 1)),
            pl.BlockSpec((POOL_GROUPS, POOL_GROUP_DIM, POOL_GROUP_DIM), lambda b, c: (0, 0, 0)),
            pl.BlockSpec((1, POOL_WIDTH), lambda b, c: (0, 0)),
        ],
        out_specs=pl.BlockSpec((rows, POOL_WIDTH), lambda b, c: (b * nc + c, 0)),
        out_shape=jax.ShapeDtypeStruct((t, POOL_WIDTH), BF16),
        scratch_shapes=[pltpu.VMEM((rows + HALO_POOL, POOL_WIDTH), F32)],
        compiler_params=pltpu.CompilerParams(
            dimension_semantics=("parallel", "arbitrary"), vmem_limit_bytes=VMEM_LIMIT),
        name="pool",
    )(proj, proj, pool_w, pool_scale)


def _merge_kernel(ya_ref, yb_ref, ga_ref, gb_ref, wa_ref, wb_ref, o_ref):
    br_a = jnp.dot(ya_ref[...], wa_ref[...], preferred_element_type=F32)
    br_b = jnp.dot(yb_ref[...], wb_ref[...], preferred_element_type=F32)
    merged = ga_ref[...].astype(F32) * br_a + gb_ref[...].astype(F32) * br_b
    o_ref[...] = merged.astype(BF16)


def _merge(y_a, y_b, proj, w_a, w_b, tm):
    t = y_a.shape[0]
    ga_block = (OFF_GATES + 2 * POOL_WIDTH) // D_MODEL
    return pl.pallas_call(
        _merge_kernel,
        grid=(t // tm,),
        in_specs=[
            pl.BlockSpec((tm, D_MODEL), lambda i: (i, 0)),
            pl.BlockSpec((tm, POOL_WIDTH), lambda i: (i, 0)),
            pl.BlockSpec((tm, D_MODEL), lambda i: (i, ga_block)),
            pl.BlockSpec((tm, D_MODEL), lambda i: (i, ga_block + 1)),
            pl.BlockSpec((D_MODEL, D_MODEL), lambda i: (0, 0)),
            pl.BlockSpec((POOL_WIDTH, D_MODEL), lambda i: (0, 0)),
        ],
        out_specs=pl.BlockSpec((tm, D_MODEL), lambda i: (i, 0)),
        out_shape=jax.ShapeDtypeStruct((t, D_MODEL), BF16),
        compiler_params=pltpu.CompilerParams(
            dimension_semantics=("parallel",), vmem_limit_bytes=VMEM_LIMIT),
        name="merge",
    )(y_a, y_b, proj, proj, w_a, w_b)


def _outproj_kernel(m_ref, w_ref, x_ref, nw_ref, o_ref):
    out = jnp.dot(m_ref[...], w_ref[...], preferred_element_type=F32)
    y = out * lax.rsqrt(jnp.mean(out * out, axis=-1, keepdims=True) + RMS_EPS) * nw_ref[...]
    o_ref[...] = x_ref[...] + y


def _outproj(merged, w_out, x2, norm_w, tm):
    t = merged.shape[0]
    return pl.pallas_call(
        _outproj_kernel,
        grid=(t // tm,),
        in_specs=[
            pl.BlockSpec((tm, D_MODEL), lambda i: (i, 0)),
            pl.BlockSpec((D_MODEL, D_MODEL), lambda i: (0, 0)),
            pl.BlockSpec((tm, D_MODEL), lambda i: (i, 0)),
            pl.BlockSpec((1, D_MODEL), lambda i: (0, 0)),
        ],
        out_specs=pl.BlockSpec((tm, D_MODEL), lambda i: (i, 0)),
        out_shape=jax.ShapeDtypeStruct((t, D_MODEL), F32),
        compiler_params=pltpu.CompilerParams(
            dimension_semantics=("parallel",), vmem_limit_bytes=VMEM_LIMIT),
        name="outproj",
    )(merged, w_out, x2, norm_w)


def _layer(x2, bsz, seq, norm_pre_w, w_in, i_bias, f_bias, conv_w, conv_b, mlstm_norm_w,
           pool_w, pool_scale, w_proj_mlstm, w_proj_pool, w_out, norm_post_w):
    w_gate = w_in[:, OFF_GATES:OFF_POOL].astype(BF16)
    bias = jnp.concatenate([i_bias, f_bias]).astype(F32)

    h, gcol, grow = _prenorm(x2, norm_pre_w[None, :], w_gate, w_gate.T, tm=512)
    proj = _proj(h, w_in, conv_w, conv_b[None, :], seq, tm=1024, tn=1024)
    y_a = _mlstm(proj, gcol, grow, bias[None, :], bias[:, None], mlstm_norm_w[None, :],
                 bsz, seq, chunk=256)
    y_b = _pool(proj, pool_w.astype(BF16), pool_scale[None, :], bsz, seq, rows=512)
    merged = _merge(y_a, y_b, proj, w_proj_mlstm.astype(BF16), w_proj_pool.astype(BF16), tm=512)
    return _outproj(merged, w_out.astype(BF16), x2, norm_post_w[None, :], tm=512)


def kernel(x, norm_pre_w, w_in, mlstm_i_bias, mlstm_f_bias, qk_conv_w, qk_conv_b, mlstm_norm_w,
           pool_w, pool_scale, w_proj_mlstm, w_proj_pool, w_out, norm_post_w):
    bsz, seq, d = x.shape
    assert d == D_MODEL and w_in.shape[-1] == MAIN_COLS + GATE_COLS
    x2 = x.reshape(bsz * seq, d)
    for l in range(norm_pre_w.shape[0]):
        x2 = _layer(x2, bsz, seq, norm_pre_w[l], w_in[l], mlstm_i_bias[l], mlstm_f_bias[l],
                    qk_conv_w[l], qk_conv_b[l], mlstm_norm_w[l], pool_w[l], pool_scale[l],
                    w_proj_mlstm[l], w_proj_pool[l], w_out[l], norm_post_w[l])
    return x2.reshape(bsz, seq, d)
```

```python
import functools

import jax
import jax.numpy as jnp
from jax import lax
from jax.experimental import pallas as pl
from jax.experimental.pallas import tpu as pltpu

D_MODEL = 2048
HEADS = 8
HEAD_DIM = D_MODEL // HEADS
CONV_WIDTH = 4
POOL_GROUPS = 4
POOL_WIDTH = D_MODEL // 2
POOL_GROUP_DIM = POOL_WIDTH // POOL_GROUPS
POOL_WINDOWS = (2, 4, 8, 16)
RMS_EPS = 1e-6
GATE_COLS = 2 * HEADS
OFF_GATES = 5 * D_MODEL
OFF_POOL = OFF_GATES + GATE_COLS
MAIN_COLS = 5 * D_MODEL + 2 * POOL_WIDTH + 2 * D_MODEL

SUBLANES = 8
LANES = 128
MXU_COLS = 256
HALO_POOL = 2 * SUBLANES
VMEM_LIMIT = 56 * 1024 * 1024

F32 = jnp.float32
BF16 = jnp.bfloat16


def _silu(x):
    return x * jax.nn.sigmoid(x)


def _log_sigmoid(x):
    return jnp.minimum(x, 0.0) - jnp.log1p(jnp.exp(-jnp.abs(x)))


def _chunk_scan(x, combine, fill, lane_in_chunk, chunk):
    sh = 1
    while sh < chunk:
        x = combine(x, jnp.where(lane_in_chunk >= sh, pltpu.roll(x, sh, 1), fill))
        sh *= 2
    return x


def _prenorm_kernel(x_ref, nw_ref, wgt_ref, bias_ref, h_ref, gates_ref, *, chunk):
    x = x_ref[...]
    y = x * lax.rsqrt(jnp.mean(x * x, axis=-1, keepdims=True) + RMS_EPS) * nw_ref[...]
    hb = y.astype(BF16)
    h_ref[...] = hb
    gr = lax.dot_general(wgt_ref[...], hb, (((1,), (1,)), ((), ())),
                         preferred_element_type=F32) + bias_ref[...]
    log_i = gr[0:HEADS, :]
    log_f = _log_sigmoid(gr[HEADS:GATE_COLS, :])
    lane_in_chunk = lax.broadcasted_iota(jnp.int32, log_i.shape, 1) % chunk
    b = _chunk_scan(log_f, jnp.add, 0.0, lane_in_chunk, chunk)
    a = log_i - b
    a_max = _chunk_scan(a, jnp.maximum, -jnp.inf, lane_in_chunk, chunk)
    gates_ref[...] = jnp.concatenate([b, a, a_max], axis=0)


def _prenorm(x2, norm_w, w_gate_t, bias_col, tm, chunk):
    t = x2.shape[0]
    return pl.pallas_call(
        functools.partial(_prenorm_kernel, chunk=chunk),
        grid=(t // tm,),
        in_specs=[
            pl.BlockSpec((tm, D_MODEL), lambda i: (i, 0)),
            pl.BlockSpec((1, D_MODEL), lambda i: (0, 0)),
            pl.BlockSpec((GATE_COLS, D_MODEL), lambda i: (0, 0)),
            pl.BlockSpec((GATE_COLS, 1), lambda i: (0, 0)),
        ],
        out_specs=[
            pl.BlockSpec((tm, D_MODEL), lambda i: (i, 0)),
            pl.BlockSpec((3 * HEADS, tm), lambda i: (0, i)),
        ],
        out_shape=[
            jax.ShapeDtypeStruct((t, D_MODEL), BF16),
            jax.ShapeDtypeStruct((3 * HEADS, t), F32),
        ],
        compiler_params=pltpu.CompilerParams(
            dimension_semantics=("parallel",), vmem_limit_bytes=VMEM_LIMIT),
        name="prenorm",
    )(x2, norm_w, w_gate_t, bias_col)


def _proj_kernel(h_ref, w_ref, wn_ref, cw_ref, cb_ref, o_ref, wb_ref, halo_ref,
                 *, tm, tn, tiles_per_seq):
    j = pl.program_id(0)
    i = pl.program_id(1)
    seg = D_MODEL // tn
    shift_tile = OFF_GATES // tn
    krows = 256

    @pl.when(jnp.logical_and(i == 0, j < shift_tile))
    def _():
        for r in range(0, D_MODEL, krows):
            wb_ref[r:r + krows, :] = w_ref[r:r + krows, :].astype(BF16)

    @pl.when(jnp.logical_and(i == 0, j >= shift_tile))
    def _():
        for r in range(0, D_MODEL, krows):
            cat = jnp.concatenate([w_ref[r:r + krows, :], wn_ref[r:r + krows, :]], axis=1)
            shifted = pltpu.roll(cat, tn + LANES - GATE_COLS, 1)
            wb_ref[r:r + krows, :] = shifted[:, :tn].astype(BF16)

    def stripes(epilogue):
        for c in range(0, tn, MXU_COLS):
            r = jnp.dot(h_ref[...], wb_ref[:, c:c + MXU_COLS], preferred_element_type=F32)
            o_ref[:, c:c + MXU_COLS] = epilogue(r, c).astype(BF16)

    is_conv = j < 2 * seg
    pool_tile = shift_tile
    is_plain = jnp.logical_or(jnp.logical_and(j >= 2 * seg, j < 3 * seg), j == pool_tile)
    is_silu = jnp.logical_or(jnp.logical_and(j >= 4 * seg, j < 5 * seg),
                             jnp.logical_and(j > pool_tile, j < pool_tile + 2 * (POOL_WIDTH // tn)))
    is_sigmoid = jnp.logical_not(jnp.logical_or(jnp.logical_or(is_conv, is_plain), is_silu))

    @pl.when(is_conv)
    def _():
        @pl.when(i % tiles_per_seq == 0)
        def _():
            halo_ref[...] = jnp.zeros_like(halo_ref)

        out_scale = jnp.where(j < seg, HEAD_DIM ** -0.5, 1.0).astype(F32)

        def epilogue(r, c):
            ext = jnp.concatenate([halo_ref[:, c:c + MXU_COLS], r], axis=0)
            acc = cw_ref[0:1, c:c + MXU_COLS] * ext
            for k in range(1, CONV_WIDTH):
                acc = cw_ref[k:k + 1, c:c + MXU_COLS] * ext + pltpu.roll(acc, 1, 0)
            halo_ref[:, c:c + MXU_COLS] = r[tm - SUBLANES:, :]
            y = acc[SUBLANES:, :] + cb_ref[:, c:c + MXU_COLS]
            return _silu(y) * out_scale

        stripes(epilogue)

    @pl.when(is_plain)
    def _():
        stripes(lambda r, c: r)

    @pl.when(is_silu)
    def _():
        stripes(lambda r, c: _silu(r))

    @pl.when(is_sigmoid)
    def _():
        stripes(lambda r, c: jax.nn.sigmoid(r))


def _proj(h, w_in, layer, conv_w, conv_b, seq, tm, tn):
    t = h.shape[0]
    n_tiles = MAIN_COLS // tn
    shift_tile = OFF_GATES // tn
    conv_tiles = 2 * D_MODEL // tn
    lanes_per_tile = tn // LANES
    conv_block = lambda j, i: (0, jnp.minimum(j, conv_tiles - 1))
    return pl.pallas_call(
        functools.partial(_proj_kernel, tm=tm, tn=tn, tiles_per_seq=seq // tm),
        grid=(n_tiles, t // tm),
        in_specs=[
            pl.BlockSpec((tm, D_MODEL), lambda j, i: (i, 0)),
            pl.BlockSpec((None, D_MODEL, tn), lambda j, i: (layer, 0, j)),
            pl.BlockSpec((None, D_MODEL, LANES),
                         lambda j, i: (layer, 0, jnp.where(j >= shift_tile, (j + 1) * lanes_per_tile, 0))),
            pl.BlockSpec((CONV_WIDTH, tn), conv_block),
            pl.BlockSpec((1, tn), conv_block),
        ],
        out_specs=pl.BlockSpec((tm, tn), lambda j, i: (i, j)),
        out_shape=jax.ShapeDtypeStruct((t, MAIN_COLS), BF16),
        scratch_shapes=[pltpu.VMEM((D_MODEL, tn), BF16), pltpu.VMEM((SUBLANES, tn), F32)],
        compiler_params=pltpu.CompilerParams(
            dimension_semantics=("arbitrary", "arbitrary"), vmem_limit_bytes=VMEM_LIMIT),
        name="proj",
    )(h, w_in, w_in, conv_w, conv_b)


def _mlstm_kernel(q_ref, k_ref, v_ref, o_ref, z_ref, gates_ref, nw_ref, y_ref,
                  caug_ref, m_ref, *, chunk):
    L = chunk
    d = HEAD_DIM

    @pl.when(pl.program_id(1) == 0)
    def _():
        caug_ref[...] = jnp.zeros_like(caug_ref)
        m_ref[...] = jnp.zeros_like(m_ref)

    b_row = gates_ref[0:HEADS, :]
    a_row = gates_ref[HEADS:2 * HEADS, :]
    a_max = gates_ref[2 * HEADS:3 * HEADS, :]
    m_st = m_ref[...]
    big_m = jnp.maximum(a_max, m_st)
    g = b_row[:, L - 1:L]
    m_new = jnp.maximum(g + m_st, g + a_max[:, L - 1:L])
    decay = jnp.exp(g + m_st - m_new)
    w_inter = jnp.exp(m_st - big_m)
    floor = jnp.exp(-(b_row + big_m))
    w_new = jnp.exp(g + a_row - m_new)
    m_ref[...] = m_new

    rows = jnp.concatenate(
        [big_m, w_inter, floor, w_new, jnp.zeros((LANES - 4 * HEADS, L), F32)], axis=0)
    cols = rows.T

    def column(j):
        return jnp.broadcast_to(cols[:, j:j + 1], (L, LANES))

    def twice(x):
        return jnp.concatenate([x, x], axis=1)

    t_idx = lax.broadcasted_iota(jnp.int32, (L, L), 0)
    s_idx = lax.broadcasted_iota(jnp.int32, (L, L), 1)
    causal = s_idx <= t_idx
    ones_blk = jnp.ones((L, LANES), BF16)
    ones_sq = jnp.ones((d, LANES), BF16)

    for hh in range(HEADS):
        col = hh * d
        qb = q_ref[:, col:col + d]
        kb = k_ref[:, col:col + d]
        vaug = jnp.concatenate([v_ref[:, col:col + d], ones_blk], axis=1)
        caug = caug_ref[hh]

        big_m_c = column(hh)
        w_inter_c = column(HEADS + hh)
        floor_c = column(2 * HEADS + hh)
        w_new_c = column(3 * HEADS + hh)

        s = lax.dot_general(qb, kb, (((1,), (1,)), ((), ())), preferred_element_type=F32)
        log_d = jnp.where(causal, a_row[hh:hh + 1, :] - twice(big_m_c), -jnp.inf)
        p = jnp.exp(log_d) * s
        pv = jnp.dot(p.astype(BF16), vaug, preferred_element_type=F32)
        qc = jnp.dot(qb, caug.astype(BF16), preferred_element_type=F32)
        num = pv[:, :d] + twice(w_inter_c) * qc[:, :d]
        den = pv[:, d:] + w_inter_c * qc[:, d:]
        inv = 1.0 / jnp.maximum(jnp.abs(den), floor_c)
        ssq = jnp.dot((num * num).astype(BF16), ones_sq, preferred_element_type=F32)
        r = inv * lax.rsqrt(ssq * (1.0 / d) * (inv * inv) + RMS_EPS)
        hn = num * twice(r) * nw_ref[:, col:col + d]
        og = o_ref[:, col:col + d].astype(F32)
        za = z_ref[:, col:col + d].astype(F32)
        y_ref[:, col:col + d] = (og * hn * za).astype(BF16)

        wk = (twice(w_new_c) * kb.astype(F32)).astype(BF16)
        upd = lax.dot_general(wk, vaug, (((0,), (0,)), ((), ())), preferred_element_type=F32)
        caug_ref[hh] = decay[hh:hh + 1, :] * caug + upd


def _mlstm(proj, gates, norm_w, bsz, seq, chunk):
    nc = seq // chunk
    t = bsz * seq
    col_block = lambda cb: pl.BlockSpec((chunk, D_MODEL), lambda b, c: (b * nc + c, cb))
    const = lambda shape: pl.BlockSpec(shape, lambda b, c: (0,) * len(shape))
    return pl.pallas_call(
        functools.partial(_mlstm_kernel, chunk=chunk),
        grid=(bsz, nc),
        in_specs=[
            col_block(0), col_block(1), col_block(2), col_block(3), col_block(4),
            pl.BlockSpec((3 * HEADS, chunk), lambda b, c: (0, b * nc + c)),
            const((1, D_MODEL)),
        ],
        out_specs=pl.BlockSpec((chunk, D_MODEL), lambda b, c: (b * nc + c, 0)),
        out_shape=jax.ShapeDtypeStruct((t, D_MODEL), BF16),
        scratch_shapes=[
            pltpu.VMEM((HEADS, HEAD_DIM, HEAD_DIM + LANES), F32),
            pltpu.VMEM((HEADS, 1), F32),
        ],
        compiler_params=pltpu.CompilerParams(
            dimension_semantics=("parallel", "arbitrary"), vmem_limit_bytes=VMEM_LIMIT),
        name="mlstm",
    )(proj, proj, proj, proj, proj, gates, norm_w)


def _pool_kernel(u_ref, z_ref, pw_ref, ps_ref, y_ref, ext_ref, *, rows):
    c = pl.program_id(1)

    @pl.when(c == 0)
    def _():
        ext_ref[0:HALO_POOL, :] = jnp.zeros((HALO_POOL, POOL_WIDTH), F32)

    ext_ref[HALO_POOL:HALO_POOL + rows, :] = u_ref[...].astype(F32)
    pos = lax.broadcasted_iota(jnp.int32, (rows, 1), 0) + c * rows + 1

    for g in range(POOL_GROUPS):
        col = g * POOL_GROUP_DIM
        win = POOL_WINDOWS[g]
        u = ext_ref[HALO_POOL:HALO_POOL + rows, col:col + POOL_GROUP_DIM]
        wsum = u
        for j in range(1, win):
            wsum = wsum + ext_ref[HALO_POOL - j:HALO_POOL - j + rows, col:col + POOL_GROUP_DIM]
        count = jnp.minimum(pos, win).astype(F32)
        pooled = wsum / count - u
        mixed = jnp.dot(pooled.astype(BF16), pw_ref[g], preferred_element_type=F32)
        mixed = mixed * ps_ref[:, col:col + POOL_GROUP_DIM]
        zb = z_ref[:, col:col + POOL_GROUP_DIM].astype(F32)
        y_ref[:, col:col + POOL_GROUP_DIM] = (mixed * zb).astype(BF16)

    ext_ref[0:HALO_POOL, :] = ext_ref[rows:rows + HALO_POOL, :]


def _pool(proj, pool_w, pool_scale, bsz, seq, rows):
    nc = seq // rows
    t = bsz * seq
    u_block = OFF_GATES // POOL_WIDTH
    return pl.pallas_call(
        functools.partial(_pool_kernel, rows=rows),
        grid=(bsz, nc),
        in_specs=[
            pl.BlockSpec((rows, POOL_WIDTH), lambda b, c: (b * nc + c, u_block)),
            pl.BlockSpec((rows, POOL_WIDTH), lambda b, c: (b * nc + c, u_block + 1)),
            pl.BlockSpec((POOL_GROUPS, POOL_GROUP_DIM, POOL_GROUP_DIM), lambda b, c: (0, 0, 0)),
            pl.BlockSpec((1, POOL_WIDTH), lambda b, c: (0, 0)),
        ],
        out_specs=pl.BlockSpec((rows, POOL_WIDTH), lambda b, c: (b * nc + c, 0)),
        out_shape=jax.ShapeDtypeStruct((t, POOL_WIDTH), BF16),
        scratch_shapes=[pltpu.VMEM((rows + HALO_POOL, POOL_WIDTH), F32)],
        compiler_params=pltpu.CompilerParams(
            dimension_semantics=("parallel", "arbitrary"), vmem_limit_bytes=VMEM_LIMIT),
        name="pool",
    )(proj, proj, pool_w, pool_scale)


def _merge_kernel(ya_ref, yb_ref, ga_ref, gb_ref, wa_ref, wb_ref, o_ref):
    br_a = jnp.dot(ya_ref[...], wa_ref[...], preferred_element_type=F32)
    br_b = jnp.dot(yb_ref[...], wb_ref[...], preferred_element_type=F32)
    merged = ga_ref[...].astype(F32) * br_a + gb_ref[...].astype(F32) * br_b
    o_ref[...] = merged.astype(BF16)


def _merge(y_a, y_b, proj, w_a, w_b, tm):
    t = y_a.shape[0]
    ga_block = (OFF_GATES + 2 * POOL_WIDTH) // D_MODEL
    return pl.pallas_call(
        _merge_kernel,
        grid=(t // tm,),
        in_specs=[
            pl.BlockSpec((tm, D_MODEL), lambda i: (i, 0)),
            pl.BlockSpec((tm, POOL_WIDTH), lambda i: (i, 0)),
            pl.BlockSpec((tm, D_MODEL), lambda i: (i, ga_block)),
            pl.BlockSpec((tm, D_MODEL), lambda i: (i, ga_block + 1)),
            pl.BlockSpec((D_MODEL, D_MODEL), lambda i: (0, 0)),
            pl.BlockSpec((POOL_WIDTH, D_MODEL), lambda i: (0, 0)),
        ],
        out_specs=pl.BlockSpec((tm, D_MODEL), lambda i: (i, 0)),
        out_shape=jax.ShapeDtypeStruct((t, D_MODEL), BF16),
        compiler_params=pltpu.CompilerParams(
            dimension_semantics=("parallel",), vmem_limit_bytes=VMEM_LIMIT),
        name="merge",
    )(y_a, y_b, proj, proj, w_a, w_b)


def _outproj_kernel(m_ref, w_ref, x_ref, nw_ref, o_ref):
    out = jnp.dot(m_ref[...], w_ref[...], preferred_element_type=F32)
    y = out * lax.rsqrt(jnp.mean(out * out, axis=-1, keepdims=True) + RMS_EPS) * nw_ref[...]
    o_ref[...] = x_ref[...] + y


def _outproj(merged, w_out, x2, norm_w, tm):
    t = merged.shape[0]
    return pl.pallas_call(
        _outproj_kernel,
        grid=(t // tm,),
        in_specs=[
            pl.BlockSpec((tm, D_MODEL), lambda i: (i, 0)),
            pl.BlockSpec((D_MODEL, D_MODEL), lambda i: (0, 0)),
            pl.BlockSpec((tm, D_MODEL), lambda i: (i, 0)),
            pl.BlockSpec((1, D_MODEL), lambda i: (0, 0)),
        ],
        out_specs=pl.BlockSpec((tm, D_MODEL), lambda i: (i, 0)),
        out_shape=jax.ShapeDtypeStruct((t, D_MODEL), F32),
        compiler_params=pltpu.CompilerParams(
            dimension_semantics=("parallel",), vmem_limit_bytes=VMEM_LIMIT),
        name="outproj",
    )(merged, w_out, x2, norm_w)


def _layer(x2, bsz, seq, layer, norm_pre_w, w_in, i_bias, f_bias, conv_w, conv_b, mlstm_norm_w,
           pool_w, pool_scale, w_proj_mlstm, w_proj_pool, w_out, norm_post_w):
    w_gate_t = w_in[layer, :, OFF_GATES:OFF_POOL].T.astype(BF16)
    bias = jnp.concatenate([i_bias, f_bias]).astype(F32)

    chunk = 256
    h, gates = _prenorm(x2, norm_pre_w[None, :], w_gate_t, bias[:, None], tm=512, chunk=chunk)
    proj = _proj(h, w_in, layer, conv_w, conv_b[None, :], seq, tm=1024, tn=1024)
    y_a = _mlstm(proj, gates, mlstm_norm_w[None, :], bsz, seq, chunk=chunk)
    y_b = _pool(proj, pool_w.astype(BF16), pool_scale[None, :], bsz, seq, rows=512)
    merged = _merge(y_a, y_b, proj, w_proj_mlstm.astype(BF16), w_proj_pool.astype(BF16), tm=512)
    return _outproj(merged, w_out.astype(BF16), x2, norm_post_w[None, :], tm=512)


def kernel(x, norm_pre_w, w_in, mlstm_i_bias, mlstm_f_bias, qk_conv_w, qk_conv_b, mlstm_norm_w,
           pool_w, pool_scale, w_proj_mlstm, w_proj_pool, w_out, norm_post_w):
    bsz, seq, d = x.shape
    assert d == D_MODEL and w_in.shape[-1] == MAIN_COLS + GATE_COLS
    x2 = x.reshape(bsz * seq, d)
    for l in range(norm_pre_w.shape[0]):
        x2 = _layer(x2, bsz, seq, l, norm_pre_w[l], w_in, mlstm_i_bias[l], mlstm_f_bias[l],
                    qk_conv_w[l], qk_conv_b[l], mlstm_norm_w[l], pool_w[l], pool_scale[l],
                    w_proj_mlstm[l], w_proj_pool[l], w_out[l], norm_post_w[l])
    return x2.reshape(bsz, seq, d)
```

```python
import functools

import jax
import jax.numpy as jnp
from jax import lax
from jax.experimental import pallas as pl
from jax.experimental.pallas import tpu as pltpu

D_MODEL = 2048
HEADS = 8
HEAD_DIM = D_MODEL // HEADS
CONV_WIDTH = 4
POOL_GROUPS = 4
POOL_WIDTH = D_MODEL // 2
POOL_GROUP_DIM = POOL_WIDTH // POOL_GROUPS
POOL_WINDOWS = (2, 4, 8, 16)
RMS_EPS = 1e-6
GATE_COLS = 2 * HEADS
OFF_GATES = 5 * D_MODEL
OFF_POOL = OFF_GATES + GATE_COLS
MAIN_COLS = 5 * D_MODEL + 2 * POOL_WIDTH + 2 * D_MODEL

SUBLANES = 8
LANES = 128
MXU_COLS = 256
HALO_POOL = 2 * SUBLANES
VMEM_LIMIT = 56 * 1024 * 1024

F32 = jnp.float32
BF16 = jnp.bfloat16


def _sigmoid(x):
    return 0.5 * jnp.tanh(0.5 * x) + 0.5


def _silu(x):
    half = 0.5 * x
    return half * jnp.tanh(half) + half


def _log_sigmoid(x):
    return jnp.minimum(x, 0.0) - jnp.log1p(jnp.exp(-jnp.abs(x)))


def _chunk_scan(x, combine, fill, lane_in_chunk, chunk):
    sh = 1
    while sh < chunk:
        x = combine(x, jnp.where(lane_in_chunk >= sh, pltpu.roll(x, sh, 1), fill))
        sh *= 2
    return x


def _prenorm_kernel(x_ref, nw_ref, wgt_ref, bias_ref, h_ref, gates_ref, *, chunk):
    x = x_ref[...]
    y = x * lax.rsqrt(jnp.mean(x * x, axis=-1, keepdims=True) + RMS_EPS) * nw_ref[...]
    hb = y.astype(BF16)
    h_ref[...] = hb
    gr = lax.dot_general(wgt_ref[...].astype(BF16), hb, (((1,), (1,)), ((), ())),
                         preferred_element_type=F32) + bias_ref[...]
    log_i = gr[0:HEADS, :]
    log_f = _log_sigmoid(gr[HEADS:GATE_COLS, :])
    lane_in_chunk = lax.broadcasted_iota(jnp.int32, log_i.shape, 1) % chunk
    b = _chunk_scan(log_f, jnp.add, 0.0, lane_in_chunk, chunk)
    a = log_i - b
    a_max = _chunk_scan(a, jnp.maximum, -jnp.inf, lane_in_chunk, chunk)
    gates_ref[...] = jnp.concatenate([b, a, a_max], axis=0)


def _prenorm(x2, norm_w, w_in_t, layer, bias_col, tm, chunk):
    t = x2.shape[0]
    return pl.pallas_call(
        functools.partial(_prenorm_kernel, chunk=chunk),
        grid=(t // tm,),
        in_specs=[
            pl.BlockSpec((tm, D_MODEL), lambda i: (i, 0)),
            pl.BlockSpec((1, D_MODEL), lambda i: (0, 0)),
            pl.BlockSpec((None, GATE_COLS, D_MODEL), lambda i: (layer, OFF_GATES // GATE_COLS, 0)),
            pl.BlockSpec((GATE_COLS, 1), lambda i: (0, 0)),
        ],
        out_specs=[
            pl.BlockSpec((tm, D_MODEL), lambda i: (i, 0)),
            pl.BlockSpec((3 * HEADS, tm), lambda i: (0, i)),
        ],
        out_shape=[
            jax.ShapeDtypeStruct((t, D_MODEL), BF16),
            jax.ShapeDtypeStruct((3 * HEADS, t), F32),
        ],
        compiler_params=pltpu.CompilerParams(
            dimension_semantics=("parallel",), vmem_limit_bytes=VMEM_LIMIT),
        name="prenorm",
    )(x2, norm_w, w_in_t, bias_col)


def _proj_kernel(h_ref, w_ref, wn_ref, cw_ref, cb_ref, o_ref, wb_ref, stage_ref,
                 *, tm, tn, tiles_per_seq):
    j = pl.program_id(0)
    i = pl.program_id(1)
    seg = D_MODEL // tn
    shift_tile = OFF_GATES // tn
    nrows = 256

    @pl.when(jnp.logical_and(i == 0, j < shift_tile))
    def _():
        for r in range(0, tn, nrows):
            wb_ref[r:r + nrows, :] = w_ref[r:r + nrows, :].astype(BF16)

    @pl.when(jnp.logical_and(i == 0, j >= shift_tile))
    def _():
        for r in range(0, tn - nrows, nrows):
            wb_ref[r:r + nrows, :] = w_ref[r + GATE_COLS:r + GATE_COLS + nrows, :].astype(BF16)
        r = tn - nrows
        wb_ref[r:tn - GATE_COLS, :] = w_ref[r + GATE_COLS:tn, :].astype(BF16)
        wb_ref[tn - GATE_COLS:tn, :] = wn_ref[...].astype(BF16)

    def matmul(c):
        return lax.dot_general(h_ref[...], wb_ref[c:c + MXU_COLS, :], (((1,), (1,)), ((), ())),
                               preferred_element_type=F32)

    def stripes(epilogue):
        for c in range(0, tn, MXU_COLS):
            o_ref[:, c:c + MXU_COLS] = epilogue(matmul(c)).astype(BF16)

    is_conv = j < 2 * seg
    pool_tile = shift_tile
    is_plain = jnp.logical_or(jnp.logical_and(j >= 2 * seg, j < 3 * seg), j == pool_tile)
    is_silu = jnp.logical_or(jnp.logical_and(j >= 4 * seg, j < 5 * seg),
                             jnp.logical_and(j > pool_tile, j < pool_tile + 2 * (POOL_WIDTH // tn)))
    is_sigmoid = jnp.logical_not(jnp.logical_or(jnp.logical_or(is_conv, is_plain), is_silu))

    @pl.when(jnp.logical_and(i == 0, j == 0))
    def _():
        stage_ref[...] = jnp.zeros_like(stage_ref)

    @pl.when(is_conv)
    def _():
        seq_start = i % tiles_per_seq == 0
        out_scale = jnp.where(j < seg, HEAD_DIM ** -0.5, 1.0).astype(F32)
        groups = tm // SUBLANES + 1
        sublane = lax.broadcasted_iota(jnp.int32, (groups, SUBLANES, MXU_COLS), 1)

        def shift_down(a):
            rot = pltpu.roll(a, 1, 1)
            above = jnp.concatenate([rot[groups - 1:], rot[:groups - 1]], axis=0)
            return jnp.where(sublane == 0, above, rot)

        def stage(s):
            history = stage_ref[s, tm:tm + SUBLANES, :]
            stage_ref[s, 0:SUBLANES, :] = jnp.where(seq_start, 0.0, history)
            stage_ref[s, SUBLANES:, :] = matmul(s * MXU_COLS)

        def finish(s):
            c = s * MXU_COLS
            ext = stage_ref[s].reshape(groups, SUBLANES, MXU_COLS)
            acc = cw_ref[0:1, c:c + MXU_COLS] * ext
            for k in range(1, CONV_WIDTH):
                acc = cw_ref[k:k + 1, c:c + MXU_COLS] * ext + shift_down(acc)
            y = acc[1:].reshape(tm, MXU_COLS) + cb_ref[:, c:c + MXU_COLS]
            o_ref[:, c:c + MXU_COLS] = (_silu(y) * out_scale).astype(BF16)

        n_stripes = tn // MXU_COLS
        stage(0)
        for s in range(1, n_stripes):
            stage(s)
            finish(s - 1)
        finish(n_stripes - 1)

    @pl.when(is_plain)
    def _():
        stripes(lambda r: r)

    @pl.when(is_silu)
    def _():
        stripes(_silu)

    @pl.when(is_sigmoid)
    def _():
        stripes(_sigmoid)


def _proj(h, w_in_t, layer, conv_w, conv_b, seq, tm, tn):
    t = h.shape[0]
    n_tiles = MAIN_COLS // tn
    shift_tile = OFF_GATES // tn
    conv_tiles = 2 * D_MODEL // tn
    gate_blocks_per_tile = tn // GATE_COLS
    conv_block = lambda j, i: (0, jnp.minimum(j, conv_tiles - 1))
    return pl.pallas_call(
        functools.partial(_proj_kernel, tm=tm, tn=tn, tiles_per_seq=seq // tm),
        grid=(n_tiles, t // tm),
        in_specs=[
            pl.BlockSpec((tm, D_MODEL), lambda j, i: (i, 0)),
            pl.BlockSpec((None, tn, D_MODEL), lambda j, i: (layer, j, 0)),
            pl.BlockSpec((None, GATE_COLS, D_MODEL),
                         lambda j, i: (layer, jnp.where(j >= shift_tile, (j + 1) * gate_blocks_per_tile, 0), 0)),
            pl.BlockSpec((CONV_WIDTH, tn), conv_block),
            pl.BlockSpec((1, tn), conv_block),
        ],
        out_specs=pl.BlockSpec((tm, tn), lambda j, i: (i, j)),
        out_shape=jax.ShapeDtypeStruct((t, MAIN_COLS), BF16),
        scratch_shapes=[pltpu.VMEM((tn, D_MODEL), BF16),
                        pltpu.VMEM((tn // MXU_COLS, tm + SUBLANES, MXU_COLS), F32)],
        compiler_params=pltpu.CompilerParams(
            dimension_semantics=("arbitrary", "arbitrary"), vmem_limit_bytes=VMEM_LIMIT),
        name="proj",
    )(h, w_in_t, w_in_t, conv_w, conv_b)


def _mlstm_kernel(q_ref, k_ref, v_ref, o_ref, z_ref, gates_ref, nw_ref, y_ref,
                  caug_ref, m_ref, *, chunk):
    L = chunk
    d = HEAD_DIM

    @pl.when(pl.program_id(1) == 0)
    def _():
        caug_ref[...] = jnp.zeros_like(caug_ref)
        m_ref[...] = jnp.zeros_like(m_ref)

    b_row = gates_ref[0:HEADS, :]
    a_row = gates_ref[HEADS:2 * HEADS, :]
    a_max = gates_ref[2 * HEADS:3 * HEADS, :]
    m_st = m_ref[...]
    big_m = jnp.maximum(a_max, m_st)
    g = b_row[:, L - 1:L]
    m_new = jnp.maximum(g + m_st, g + a_max[:, L - 1:L])
    decay = jnp.exp(g + m_st - m_new)
    w_inter = jnp.exp(m_st - big_m)
    floor = jnp.exp(-(b_row + big_m))
    w_new = jnp.exp(g + a_row - m_new)
    m_ref[...] = m_new

    rows = jnp.concatenate(
        [big_m, w_inter, floor, w_new, jnp.zeros((LANES - 4 * HEADS, L), F32)], axis=0)
    cols = rows.T

    def column(j):
        return jnp.broadcast_to(cols[:, j:j + 1], (L, LANES))

    def twice(x):
        return jnp.concatenate([x, x], axis=1)

    t_idx = lax.broadcasted_iota(jnp.int32, (L, L), 0)
    s_idx = lax.broadcasted_iota(jnp.int32, (L, L), 1)
    causal = s_idx <= t_idx
    ones_blk = jnp.ones((L, LANES), BF16)
    ones_sq = jnp.ones((d, LANES), BF16)

    for hh in range(HEADS):
        col = hh * d
        qb = q_ref[:, col:col + d]
        kb = k_ref[:, col:col + d]
        vaug = jnp.concatenate([v_ref[:, col:col + d], ones_blk], axis=1)
        caug = caug_ref[hh]

        big_m_c = column(hh)
        w_inter_c = column(HEADS + hh)
        floor_c = column(2 * HEADS + hh)
        w_new_c = column(3 * HEADS + hh)

        s = lax.dot_general(qb, kb, (((1,), (1,)), ((), ())), preferred_element_type=F32)
        log_d = jnp.where(causal, a_row[hh:hh + 1, :] - twice(big_m_c), -jnp.inf)
        p = jnp.exp(log_d) * s
        pv = jnp.dot(p.astype(BF16), vaug, preferred_element_type=F32)
        qc = jnp.dot(qb, caug.astype(BF16), preferred_element_type=F32)
        num = pv[:, :d] + twice(w_inter_c) * qc[:, :d]
        den = pv[:, d:] + w_inter_c * qc[:, d:]
        inv = 1.0 / jnp.maximum(jnp.abs(den), floor_c)
        ssq = jnp.dot((num * num).astype(BF16), ones_sq, preferred_element_type=F32)
        r = inv * lax.rsqrt(ssq * (1.0 / d) * (inv * inv) + RMS_EPS)
        hn = num * twice(r) * nw_ref[:, col:col + d]
        og = o_ref[:, col:col + d].astype(F32)
        za = z_ref[:, col:col + d].astype(F32)
        y_ref[:, col:col + d] = (og * hn * za).astype(BF16)

        wk = (twice(w_new_c) * kb.astype(F32)).astype(BF16)
        upd = lax.dot_general(wk, vaug, (((0,), (0,)), ((), ())), preferred_element_type=F32)
        caug_ref[hh] = decay[hh:hh + 1, :] * caug + upd


def _mlstm(proj, gates, norm_w, bsz, seq, chunk):
    nc = seq // chunk
    t = bsz * seq
    col_block = lambda cb: pl.BlockSpec((chunk, D_MODEL), lambda b, c: (b * nc + c, cb))
    const = lambda shape: pl.BlockSpec(shape, lambda b, c: (0,) * len(shape))
    return pl.pallas_call(
        functools.partial(_mlstm_kernel, chunk=chunk),
        grid=(bsz, nc),
        in_specs=[
            col_block(0), col_block(1), col_block(2), col_block(3), col_block(4),
            pl.BlockSpec((3 * HEADS, chunk), lambda b, c: (0, b * nc + c)),
            const((1, D_MODEL)),
        ],
        out_specs=pl.BlockSpec((chunk, D_MODEL), lambda b, c: (b * nc + c, 0)),
        out_shape=jax.ShapeDtypeStruct((t, D_MODEL), BF16),
        scratch_shapes=[
            pltpu.VMEM((HEADS, HEAD_DIM, HEAD_DIM + LANES), F32),
            pltpu.VMEM((HEADS, 1), F32),
        ],
        compiler_params=pltpu.CompilerParams(
            dimension_semantics=("parallel", "arbitrary"), vmem_limit_bytes=VMEM_LIMIT),
        name="mlstm",
    )(proj, proj, proj, proj, proj, gates, norm_w)


def _pool_kernel(u_ref, z_ref, pw_ref, ps_ref, y_ref, ext_ref, *, rows):
    c = pl.program_id(1)

    @pl.when(c == 0)
    def _():
        ext_ref[0:HALO_POOL, :] = jnp.zeros((HALO_POOL, POOL_WIDTH), F32)

    ext_ref[HALO_POOL:HALO_POOL + rows, :] = u_ref[...].astype(F32)
    pos = lax.broadcasted_iota(jnp.int32, (rows, 1), 0) + c * rows + 1

    for g in range(POOL_GROUPS):
        col = g * POOL_GROUP_DIM
        win = POOL_WINDOWS[g]
        u = ext_ref[HALO_POOL:HALO_POOL + rows, col:col + POOL_GROUP_DIM]
        wsum = u
        for j in range(1, win):
            wsum = wsum + ext_ref[HALO_POOL - j:HALO_POOL - j + rows, col:col + POOL_GROUP_DIM]
        count = jnp.minimum(pos, win).astype(F32)
        pooled = wsum / count - u
        mixed = jnp.dot(pooled.astype(BF16), pw_ref[g], preferred_element_type=F32)
        mixed = mixed * ps_ref[:, col:col + POOL_GROUP_DIM]
        zb = z_ref[:, col:col + POOL_GROUP_DIM].astype(F32)
        y_ref[:, col:col + POOL_GROUP_DIM] = (mixed * zb).astype(BF16)

    ext_ref[0:HALO_POOL, :] = ext_ref[rows:rows + HALO_POOL, :]


def _pool(proj, pool_w, pool_scale, bsz, seq, rows):
    nc = seq // rows
    t = bsz * seq
    u_block = OFF_GATES // POOL_WIDTH
    return pl.pallas_call(
        functools.partial(_pool_kernel, rows=rows),
        grid=(bsz, nc),
        in_specs=[
            pl.BlockSpec((rows, POOL_WIDTH), lambda b, c: (b * nc + c, u_block)),
            pl.BlockSpec((rows, POOL_WIDTH), lambda b, c: (b * nc + c, u_block + 1)),
            pl.BlockSpec((POOL_GROUPS, POOL_GROUP_DIM, POOL_GROUP_DIM), lambda b, c: (0, 0, 0)),
            pl.BlockSpec((1, POOL_WIDTH), lambda b, c: (0, 0)),
        ],
        out_specs=pl.BlockSpec((rows, POOL_WIDTH), lambda b, c: (b * nc + c, 0)),
        out_shape=jax.ShapeDtypeStruct((t, POOL_WIDTH), BF16),
        scratch_shapes=[pltpu.VMEM((rows + HALO_POOL, POOL_WIDTH), F32)],
        compiler_params=pltpu.CompilerParams(
            dimension_semantics=("parallel", "arbitrary"), vmem_limit_bytes=VMEM_LIMIT),
        name="pool",
    )(proj, proj, pool_w, pool_scale)


def _merge_kernel(ya_ref, yb_ref, ga_ref, gb_ref, wa_ref, wb_ref, o_ref):
    br_a = jnp.dot(ya_ref[...], wa_ref[...], preferred_element_type=F32)
    br_b = jnp.dot(yb_ref[...], wb_ref[...], preferred_element_type=F32)
    merged = ga_ref[...].astype(F32) * br_a + gb_ref[...].astype(F32) * br_b
    o_ref[...] = merged.astype(BF16)


def _merge(y_a, y_b, proj, w_a, w_b, tm):
    t = y_a.shape[0]
    ga_block = (OFF_GATES + 2 * POOL_WIDTH) // D_MODEL
    return pl.pallas_call(
        _merge_kernel,
        grid=(t // tm,),
        in_specs=[
            pl.BlockSpec((tm, D_MODEL), lambda i: (i, 0)),
            pl.BlockSpec((tm, POOL_WIDTH), lambda i: (i, 0)),
            pl.BlockSpec((tm, D_MODEL), lambda i: (i, ga_block)),
            pl.BlockSpec((tm, D_MODEL), lambda i: (i, ga_block + 1)),
            pl.BlockSpec((D_MODEL, D_MODEL), lambda i: (0, 0)),
            pl.BlockSpec((POOL_WIDTH, D_MODEL), lambda i: (0, 0)),
        ],
        out_specs=pl.BlockSpec((tm, D_MODEL), lambda i: (i, 0)),
        out_shape=jax.ShapeDtypeStruct((t, D_MODEL), BF16),
        compiler_params=pltpu.CompilerParams(
            dimension_semantics=("parallel",), vmem_limit_bytes=VMEM_LIMIT),
        name="merge",
    )(y_a, y_b, proj, proj, w_a, w_b)


def _outproj_kernel(m_ref, w_ref, x_ref, nw_ref, o_ref):
    out = jnp.dot(m_ref[...], w_ref[...], preferred_element_type=F32)
    y = out * lax.rsqrt(jnp.mean(out * out, axis=-1, keepdims=True) + RMS_EPS) * nw_ref[...]
    o_ref[...] = x_ref[...] + y


def _outproj(merged, w_out, x2, norm_w, tm):
    t = merged.shape[0]
    return pl.pallas_call(
        _outproj_kernel,
        grid=(t // tm,),
        in_specs=[
            pl.BlockSpec((tm, D_MODEL), lambda i: (i, 0)),
            pl.BlockSpec((D_MODEL, D_MODEL), lambda i: (0, 0)),
            pl.BlockSpec((tm, D_MODEL), lambda i: (i, 0)),
            pl.BlockSpec((1, D_MODEL), lambda i: (0, 0)),
        ],
        out_specs=pl.BlockSpec((tm, D_MODEL), lambda i: (i, 0)),
        out_shape=jax.ShapeDtypeStruct((t, D_MODEL), F32),
        compiler_params=pltpu.CompilerParams(
            dimension_semantics=("parallel",), vmem_limit_bytes=VMEM_LIMIT),
        name="outproj",
    )(merged, w_out, x2, norm_w)


def _layer(x2, bsz, seq, layer, norm_pre_w, w_in, i_bias, f_bias, conv_w, conv_b, mlstm_norm_w,
           pool_w, pool_scale, w_proj_mlstm, w_proj_pool, w_out, norm_post_w):
    w_in_t = jnp.swapaxes(w_in, 1, 2)
    bias = jnp.concatenate([i_bias, f_bias]).astype(F32)

    chunk = 256
    h, gates = _prenorm(x2, norm_pre_w[None, :], w_in_t, layer, bias[:, None], tm=512, chunk=chunk)
    proj = _proj(h, w_in_t, layer, conv_w, conv_b[None, :], seq, tm=1024, tn=1024)
    y_a = _mlstm(proj, gates, mlstm_norm_w[None, :], bsz, seq, chunk=chunk)
    y_b = _pool(proj, pool_w.astype(BF16), pool_scale[None, :], bsz, seq, rows=512)
    merged = _merge(y_a, y_b, proj, w_proj_mlstm.astype(BF16), w_proj_pool.astype(BF16), tm=512)
    return _outproj(merged, w_out.astype(BF16), x2, norm_post_w[None, :], tm=512)


def kernel(x, norm_pre_w, w_in, mlstm_i_bias, mlstm_f_bias, qk_conv_w, qk_conv_b, mlstm_norm_w,
           pool_w, pool_scale, w_proj_mlstm, w_proj_pool, w_out, norm_post_w):
    bsz, seq, d = x.shape
    assert d == D_MODEL and w_in.shape[-1] == MAIN_COLS + GATE_COLS
    x2 = x.reshape(bsz * seq, d)
    for l in range(norm_pre_w.shape[0]):
        x2 = _layer(x2, bsz, seq, l, norm_pre_w[l], w_in, mlstm_i_bias[l], mlstm_f_bias[l],
                    qk_conv_w[l], qk_conv_b[l], mlstm_norm_w[l], pool_w[l], pool_scale[l],
                    w_proj_mlstm[l], w_proj_pool[l], w_out[l], norm_post_w[l])
    return x2.reshape(bsz, seq, d)
```

```python
import functools

import jax
import jax.numpy as jnp
from jax import lax
from jax.experimental import pallas as pl
from jax.experimental.pallas import tpu as pltpu

D_MODEL = 2048
HEADS = 8
HEAD_DIM = D_MODEL // HEADS
CONV_WIDTH = 4
POOL_GROUPS = 4
POOL_WIDTH = D_MODEL // 2
POOL_GROUP_DIM = POOL_WIDTH // POOL_GROUPS
POOL_WINDOWS = (2, 4, 8, 16)
RMS_EPS = 1e-6
GATE_COLS = 2 * HEADS
OFF_GATES = 5 * D_MODEL
OFF_POOL = OFF_GATES + GATE_COLS
MAIN_COLS = 5 * D_MODEL + 2 * POOL_WIDTH + 2 * D_MODEL

SUBLANES = 8
LANES = 128
MXU_COLS = 256
HALO_POOL = 2 * SUBLANES
assert all(w & (w - 1) == 0 and w <= HALO_POOL for w in POOL_WINDOWS)
VMEM_LIMIT = 56 * 1024 * 1024
HEAD_SKEW = 1

F32 = jnp.float32
BF16 = jnp.bfloat16


def _sigmoid(x):
    return 0.5 * jnp.tanh(0.5 * x) + 0.5


def _silu(x):
    half = 0.5 * x
    return half * jnp.tanh(half) + half


def _log_sigmoid(x):
    return jnp.minimum(x, 0.0) - jnp.log1p(jnp.exp(-jnp.abs(x)))


def _chunk_scan(x, combine, fill, lane_in_chunk, chunk):
    sh = 1
    while sh < chunk:
        x = combine(x, jnp.where(lane_in_chunk >= sh, pltpu.roll(x, sh, 1), fill))
        sh *= 2
    return x


def _prenorm_kernel(x_ref, nw_ref, wgt_ref, bias_ref, h_ref, gates_ref, *, chunk):
    x = x_ref[...]
    y = x * lax.rsqrt(jnp.mean(x * x, axis=-1, keepdims=True) + RMS_EPS) * nw_ref[...]
    hb = y.astype(BF16)
    h_ref[...] = hb
    gr = lax.dot_general(wgt_ref[...].astype(BF16), hb, (((1,), (1,)), ((), ())),
                         preferred_element_type=F32) + bias_ref[...]
    log_i = gr[0:HEADS, :]
    log_f = _log_sigmoid(gr[HEADS:GATE_COLS, :])
    lane_in_chunk = lax.broadcasted_iota(jnp.int32, log_i.shape, 1) % chunk
    b = _chunk_scan(log_f, jnp.add, 0.0, lane_in_chunk, chunk)
    a = log_i - b
    a_max = _chunk_scan(a, jnp.maximum, -jnp.inf, lane_in_chunk, chunk)
    gates_ref[...] = jnp.concatenate([b, a, a_max], axis=0)


def _prenorm(x2, norm_w, w_in_t, layer, bias_col, tm, chunk):
    t = x2.shape[0]
    return pl.pallas_call(
        functools.partial(_prenorm_kernel, chunk=chunk),
        grid=(t // tm,),
        in_specs=[
            pl.BlockSpec((tm, D_MODEL), lambda i: (i, 0)),
            pl.BlockSpec((1, D_MODEL), lambda i: (0, 0)),
            pl.BlockSpec((None, GATE_COLS, D_MODEL), lambda i: (layer, OFF_GATES // GATE_COLS, 0)),
            pl.BlockSpec((GATE_COLS, 1), lambda i: (0, 0)),
        ],
        out_specs=[
            pl.BlockSpec((tm, D_MODEL), lambda i: (i, 0)),
            pl.BlockSpec((3 * HEADS, tm), lambda i: (0, i)),
        ],
        out_shape=[
            jax.ShapeDtypeStruct((t, D_MODEL), BF16),
            jax.ShapeDtypeStruct((3 * HEADS, t), F32),
        ],
        compiler_params=pltpu.CompilerParams(
            dimension_semantics=("parallel",), vmem_limit_bytes=VMEM_LIMIT),
        name="prenorm",
    )(x2, norm_w, w_in_t, bias_col)


def _proj_kernel(h_ref, w_ref, wn_ref, cw_ref, cb_ref, o_ref, wb_ref, stage_ref,
                 *, tm, tn, tiles_per_seq):
    j = pl.program_id(0)
    i = pl.program_id(1)
    seg = D_MODEL // tn
    shift_tile = OFF_GATES // tn
    nrows = 256

    @pl.when(jnp.logical_and(i == 0, j < shift_tile))
    def _():
        for r in range(0, tn, nrows):
            wb_ref[r:r + nrows, :] = w_ref[r:r + nrows, :].astype(BF16)

    @pl.when(jnp.logical_and(i == 0, j >= shift_tile))
    def _():
        for r in range(0, tn - nrows, nrows):
            wb_ref[r:r + nrows, :] = w_ref[r + GATE_COLS:r + GATE_COLS + nrows, :].astype(BF16)
        r = tn - nrows
        wb_ref[r:tn - GATE_COLS, :] = w_ref[r + GATE_COLS:tn, :].astype(BF16)
        wb_ref[tn - GATE_COLS:tn, :] = wn_ref[...].astype(BF16)

    def matmul(c):
        return lax.dot_general(h_ref[...], wb_ref[c:c + MXU_COLS, :], (((1,), (1,)), ((), ())),
                               preferred_element_type=F32)

    def stripes(epilogue):
        for c in range(0, tn, MXU_COLS):
            o_ref[:, c:c + MXU_COLS] = epilogue(matmul(c)).astype(BF16)

    is_conv = j < 2 * seg
    pool_tile = shift_tile
    is_plain = jnp.logical_or(jnp.logical_and(j >= 2 * seg, j < 3 * seg), j == pool_tile)
    is_silu = jnp.logical_or(jnp.logical_and(j >= 4 * seg, j < 5 * seg),
                             jnp.logical_and(j > pool_tile, j < pool_tile + 2 * (POOL_WIDTH // tn)))
    is_sigmoid = jnp.logical_not(jnp.logical_or(jnp.logical_or(is_conv, is_plain), is_silu))

    @pl.when(jnp.logical_and(i == 0, j == 0))
    def _():
        stage_ref[...] = jnp.zeros_like(stage_ref)

    @pl.when(is_conv)
    def _():
        seq_start = i % tiles_per_seq == 0
        out_scale = jnp.where(j < seg, HEAD_DIM ** -0.5, 1.0).astype(F32)
        groups = tm // SUBLANES + 1
        sublane = lax.broadcasted_iota(jnp.int32, (groups, SUBLANES, MXU_COLS), 1)

        def shift_down(a):
            rot = pltpu.roll(a, 1, 1)
            above = jnp.concatenate([rot[groups - 1:], rot[:groups - 1]], axis=0)
            return jnp.where(sublane == 0, above, rot)

        def stage(s):
            history = stage_ref[s, tm:tm + SUBLANES, :]
            stage_ref[s, 0:SUBLANES, :] = jnp.where(seq_start, 0.0, history)
            stage_ref[s, SUBLANES:, :] = matmul(s * MXU_COLS)

        def finish(s):
            c = s * MXU_COLS
            ext = stage_ref[s].reshape(groups, SUBLANES, MXU_COLS)
            acc = cw_ref[0:1, c:c + MXU_COLS] * ext
            for k in range(1, CONV_WIDTH):
                acc = cw_ref[k:k + 1, c:c + MXU_COLS] * ext + shift_down(acc)
            y = acc[1:].reshape(tm, MXU_COLS) + cb_ref[:, c:c + MXU_COLS]
            o_ref[:, c:c + MXU_COLS] = (_silu(y) * out_scale).astype(BF16)

        n_stripes = tn // MXU_COLS
        stage(0)
        for s in range(1, n_stripes):
            stage(s)
            finish(s - 1)
        finish(n_stripes - 1)

    @pl.when(is_plain)
    def _():
        stripes(lambda r: r)

    @pl.when(is_silu)
    def _():
        stripes(_silu)

    @pl.when(is_sigmoid)
    def _():
        stripes(_sigmoid)


def _proj(h, w_in_t, layer, conv_w, conv_b, seq, tm, tn):
    t = h.shape[0]
    n_tiles = MAIN_COLS // tn
    shift_tile = OFF_GATES // tn
    conv_tiles = 2 * D_MODEL // tn
    gate_blocks_per_tile = tn // GATE_COLS
    conv_block = lambda j, i: (0, jnp.minimum(j, conv_tiles - 1))
    return pl.pallas_call(
        functools.partial(_proj_kernel, tm=tm, tn=tn, tiles_per_seq=seq // tm),
        grid=(n_tiles, t // tm),
        in_specs=[
            pl.BlockSpec((tm, D_MODEL), lambda j, i: (i, 0)),
            pl.BlockSpec((None, tn, D_MODEL), lambda j, i: (layer, j, 0)),
            pl.BlockSpec((None, GATE_COLS, D_MODEL),
                         lambda j, i: (layer, jnp.where(j >= shift_tile, (j + 1) * gate_blocks_per_tile, 0), 0)),
            pl.BlockSpec((CONV_WIDTH, tn), conv_block),
            pl.BlockSpec((1, tn), conv_block),
        ],
        out_specs=pl.BlockSpec((tm, tn), lambda j, i: (i, j)),
        out_shape=jax.ShapeDtypeStruct((t, MAIN_COLS), BF16),
        scratch_shapes=[pltpu.VMEM((tn, D_MODEL), BF16),
                        pltpu.VMEM((tn // MXU_COLS, tm + SUBLANES, MXU_COLS), F32)],
        compiler_params=pltpu.CompilerParams(
            dimension_semantics=("arbitrary", "arbitrary"), vmem_limit_bytes=VMEM_LIMIT),
        name="proj",
    )(h, w_in_t, w_in_t, conv_w, conv_b)


def _mlstm_kernel(q_ref, k_ref, v_ref, o_ref, z_ref, gates_ref, nw_ref, y_ref,
                  caug_ref, m_ref, *, chunk):
    L = chunk
    d = HEAD_DIM

    @pl.when(pl.program_id(1) == 0)
    def _():
        caug_ref[...] = jnp.zeros_like(caug_ref)
        m_ref[...] = jnp.zeros_like(m_ref)

    b_row = gates_ref[0:HEADS, :]
    a_row = gates_ref[HEADS:2 * HEADS, :]
    a_max = gates_ref[2 * HEADS:3 * HEADS, :]
    m_st = m_ref[...]
    big_m = jnp.maximum(a_max, m_st)
    g = b_row[:, L - 1:L]
    m_new = jnp.maximum(g + m_st, g + a_max[:, L - 1:L])
    decay = jnp.exp(g + m_st - m_new)
    w_inter = jnp.exp(m_st - big_m)
    floor = jnp.exp(-(b_row + big_m))
    w_new = jnp.exp(g + a_row - m_new)
    m_ref[...] = m_new

    rows = jnp.concatenate(
        [big_m, w_inter, floor, w_new, jnp.zeros((LANES - 4 * HEADS, L), F32)], axis=0)
    cols = rows.T

    def column(j):
        return jnp.broadcast_to(cols[:, j:j + 1], (L, LANES))

    def twice(x):
        return jnp.concatenate([x, x], axis=1)

    t_idx = lax.broadcasted_iota(jnp.int32, (L, L), 0)
    s_idx = lax.broadcasted_iota(jnp.int32, (L, L), 1)
    causal = s_idx <= t_idx
    ones_blk = jnp.ones((L, LANES), BF16)
    ones_sq = jnp.ones((d, LANES), BF16)

    def head_stages(hh):
        col = hh * d
        v = {}

        def load_and_score():
            v["qb"] = q_ref[:, col:col + d]
            v["kb"] = k_ref[:, col:col + d]
            v["vaug"] = jnp.concatenate([v_ref[:, col:col + d], ones_blk], axis=1)
            v["caug"] = caug_ref[hh]
            v["s"] = lax.dot_general(v["qb"], v["kb"], (((1,), (1,)), ((), ())),
                                     preferred_element_type=F32)
            v["qc"] = jnp.dot(v["qb"], v["caug"].astype(BF16),
                              preferred_element_type=F32)

        def intra():
            log_d = jnp.where(causal, a_row[hh:hh + 1, :] - twice(column(hh)), -jnp.inf)
            p = jnp.exp(log_d) * v.pop("s")
            v["pv"] = jnp.dot(p.astype(BF16), v["vaug"], preferred_element_type=F32)

        def combine():
            w_inter_c = column(HEADS + hh)
            pv, qc = v.pop("pv"), v.pop("qc")
            num = pv[:, :d] + twice(w_inter_c) * qc[:, :d]
            den = pv[:, d:] + w_inter_c * qc[:, d:]
            v["inv"] = 1.0 / jnp.maximum(jnp.abs(den), column(2 * HEADS + hh))
            v["ssq"] = jnp.dot((num * num).astype(BF16), ones_sq, preferred_element_type=F32)
            v["num"] = num

        def emit():
            inv = v.pop("inv")
            r = inv * lax.rsqrt(v.pop("ssq") * (1.0 / d) * (inv * inv) + RMS_EPS)
            hn = v.pop("num") * twice(r) * nw_ref[:, col:col + d]
            og = o_ref[:, col:col + d].astype(F32)
            za = z_ref[:, col:col + d].astype(F32)
            y_ref[:, col:col + d] = (og * hn * za).astype(BF16)

        def carry():
            wk = (twice(column(3 * HEADS + hh)) * v.pop("kb").astype(F32)).astype(BF16)
            upd = lax.dot_general(wk, v.pop("vaug"), (((0,), (0,)), ((), ())),
                                  preferred_element_type=F32)
            caug_ref[hh] = decay[hh:hh + 1, :] * v.pop("caug") + upd

        return (load_and_score, intra, combine, emit, carry)

    schedule = sorted((hh * HEAD_SKEW + k, hh, k) for hh in range(HEADS) for k in range(5))
    all_stages = [head_stages(hh) for hh in range(HEADS)]
    for _, hh, k in schedule:
        all_stages[hh][k]()


def _mlstm(proj, gates, norm_w, bsz, seq, chunk):
    nc = seq // chunk
    t = bsz * seq
    col_block = lambda cb: pl.BlockSpec((chunk, D_MODEL), lambda b, c: (b * nc + c, cb))
    const = lambda shape: pl.BlockSpec(shape, lambda b, c: (0,) * len(shape))
    return pl.pallas_call(
        functools.partial(_mlstm_kernel, chunk=chunk),
        grid=(bsz, nc),
        in_specs=[
            col_block(0), col_block(1), col_block(2), col_block(3), col_block(4),
            pl.BlockSpec((3 * HEADS, chunk), lambda b, c: (0, b * nc + c)),
            const((1, D_MODEL)),
        ],
        out_specs=pl.BlockSpec((chunk, D_MODEL), lambda b, c: (b * nc + c, 0)),
        out_shape=jax.ShapeDtypeStruct((t, D_MODEL), BF16),
        scratch_shapes=[
            pltpu.VMEM((HEADS, HEAD_DIM, HEAD_DIM + LANES), F32),
            pltpu.VMEM((HEADS, 1), F32),
        ],
        compiler_params=pltpu.CompilerParams(
            dimension_semantics=("parallel", "arbitrary"), vmem_limit_bytes=VMEM_LIMIT),
        name="mlstm",
    )(proj, proj, proj, proj, proj, gates, norm_w)


def _pool_kernel(u_ref, z_ref, pw_ref, ps_ref, y_ref, ext_ref, *, rows):
    c = pl.program_id(1)

    @pl.when(c == 0)
    def _():
        ext_ref[0:HALO_POOL, :] = jnp.zeros((HALO_POOL, POOL_WIDTH), F32)

    ext_ref[HALO_POOL:HALO_POOL + rows, :] = u_ref[...].astype(F32)
    pos = lax.broadcasted_iota(jnp.int32, (rows, 1), 0) + c * rows + 1
    history_groups = HALO_POOL // SUBLANES
    groups = rows // SUBLANES + history_groups
    sublane = lax.broadcasted_iota(jnp.int32, (groups, SUBLANES, POOL_GROUP_DIM), 1)

    def shift_down(a, j):
        if j % SUBLANES == 0:
            k = j // SUBLANES
            return jnp.concatenate([a[groups - k:], a[:groups - k]], axis=0)
        rot = pltpu.roll(a, j, 1)
        above = jnp.concatenate([rot[groups - 1:], rot[:groups - 1]], axis=0)
        return jnp.where(sublane < j, above, rot)

    for g in range(POOL_GROUPS):
        col = g * POOL_GROUP_DIM
        win = POOL_WINDOWS[g]
        ext = ext_ref[:, col:col + POOL_GROUP_DIM].reshape(groups, SUBLANES, POOL_GROUP_DIM)
        wsum, width = ext, 1
        while width < win:
            wsum = wsum + shift_down(wsum, width)
            width *= 2
        u = ext[history_groups:].reshape(rows, POOL_GROUP_DIM)
        wsum = wsum[history_groups:].reshape(rows, POOL_GROUP_DIM)
        count = jnp.minimum(pos, win).astype(F32)
        pooled = wsum / count - u
        mixed = jnp.dot(pooled.astype(BF16), pw_ref[g], preferred_element_type=F32)
        mixed = mixed * ps_ref[:, col:col + POOL_GROUP_DIM]
        zb = z_ref[:, col:col + POOL_GROUP_DIM].astype(F32)
        y_ref[:, col:col + POOL_GROUP_DIM] = (mixed * zb).astype(BF16)

    ext_ref[0:HALO_POOL, :] = ext_ref[rows:rows + HALO_POOL, :]


def _pool(proj, pool_w, pool_scale, bsz, seq, rows):
    nc = seq // rows
    t = bsz * seq
    u_block = OFF_GATES // POOL_WIDTH
    return pl.pallas_call(
        functools.partial(_pool_kernel, rows=rows),
        grid=(bsz, nc),
        in_specs=[
            pl.BlockSpec((rows, POOL_WIDTH), lambda b, c: (b * nc + c, u_block)),
            pl.BlockSpec((rows, POOL_WIDTH), lambda b, c: (b * nc + c, u_block + 1)),
            pl.BlockSpec((POOL_GROUPS, POOL_GROUP_DIM, POOL_GROUP_DIM), lambda b, c: (0, 0, 0)),
            pl.BlockSpec((1, POOL_WIDTH), lambda b, c: (0, 0)),
        ],
        out_specs=pl.BlockSpec((rows, POOL_WIDTH), lambda b, c: (b * nc + c, 0)),
        out_shape=jax.ShapeDtypeStruct((t, POOL_WIDTH), BF16),
        scratch_shapes=[pltpu.VMEM((rows + HALO_POOL, POOL_WIDTH), F32)],
        compiler_params=pltpu.CompilerParams(
            dimension_semantics=("parallel", "arbitrary"), vmem_limit_bytes=VMEM_LIMIT),
        name="pool",
    )(proj, proj, pool_w, pool_scale)


def _merge_kernel(ya_ref, yb_ref, ga_ref, gb_ref, wa_ref, wb_ref, o_ref):
    br_a = jnp.dot(ya_ref[...], wa_ref[...], preferred_element_type=F32)
    br_b = jnp.dot(yb_ref[...], wb_ref[...], preferred_element_type=F32)
    merged = ga_ref[...].astype(F32) * br_a + gb_ref[...].astype(F32) * br_b
    o_ref[...] = merged.astype(BF16)


def _merge(y_a, y_b, proj, w_a, w_b, tm):
    t = y_a.shape[0]
    ga_block = (OFF_GATES + 2 * POOL_WIDTH) // D_MODEL
    return pl.pallas_call(
        _merge_kernel,
        grid=(t // tm,),
        in_specs=[
            pl.BlockSpec((tm, D_MODEL), lambda i: (i, 0)),
            pl.BlockSpec((tm, POOL_WIDTH), lambda i: (i, 0)),
            pl.BlockSpec((tm, D_MODEL), lambda i: (i, ga_block)),
            pl.BlockSpec((tm, D_MODEL), lambda i: (i, ga_block + 1)),
            pl.BlockSpec((D_MODEL, D_MODEL), lambda i: (0, 0)),
            pl.BlockSpec((POOL_WIDTH, D_MODEL), lambda i: (0, 0)),
        ],
        out_specs=pl.BlockSpec((tm, D_MODEL), lambda i: (i, 0)),
        out_shape=jax.ShapeDtypeStruct((t, D_MODEL), BF16),
        compiler_params=pltpu.CompilerParams(
            dimension_semantics=("parallel",), vmem_limit_bytes=VMEM_LIMIT),
        name="merge",
    )(y_a, y_b, proj, proj, w_a, w_b)


def _outproj_kernel(m_ref, w_ref, x_ref, nw_ref, o_ref):
    out = jnp.dot(m_ref[...], w_ref[...], preferred_element_type=F32)
    y = out * lax.rsqrt(jnp.mean(out * out, axis=-1, keepdims=True) + RMS_EPS) * nw_ref[...]
    o_ref[...] = x_ref[...] + y


def _outproj(merged, w_out, x2, norm_w, tm):
    t = merged.shape[0]
    return pl.pallas_call(
        _outproj_kernel,
        grid=(t // tm,),
        in_specs=[
            pl.BlockSpec((tm, D_MODEL), lambda i: (i, 0)),
            pl.BlockSpec((D_MODEL, D_MODEL), lambda i: (0, 0)),
            pl.BlockSpec((tm, D_MODEL), lambda i: (i, 0)),
            pl.BlockSpec((1, D_MODEL), lambda i: (0, 0)),
        ],
        out_specs=pl.BlockSpec((tm, D_MODEL), lambda i: (i, 0)),
        out_shape=jax.ShapeDtypeStruct((t, D_MODEL), F32),
        compiler_params=pltpu.CompilerParams(
            dimension_semantics=("parallel",), vmem_limit_bytes=VMEM_LIMIT),
        name="outproj",
    )(merged, w_out, x2, norm_w)


def _layer(x2, bsz, seq, layer, norm_pre_w, w_in, i_bias, f_bias, conv_w, conv_b, mlstm_norm_w,
           pool_w, pool_scale, w_proj_mlstm, w_proj_pool, w_out, norm_post_w):
    w_in_t = jnp.swapaxes(w_in, 1, 2)
    bias = jnp.concatenate([i_bias, f_bias]).astype(F32)

    chunk = 256
    h, gates = _prenorm(x2, norm_pre_w[None, :], w_in_t, layer, bias[:, None], tm=1024, chunk=chunk)
    proj = _proj(h, w_in_t, layer, conv_w, conv_b[None, :], seq, tm=1024, tn=1024)
    y_a = _mlstm(proj, gates, mlstm_norm_w[None, :], bsz, seq, chunk=chunk)
    y_b = _pool(proj, pool_w.astype(BF16), pool_scale[None, :], bsz, seq, rows=512)
    merged = _merge(y_a, y_b, proj, w_proj_mlstm.astype(BF16), w_proj_pool.astype(BF16), tm=512)
    return _outproj(merged, w_out.astype(BF16), x2, norm_post_w[None, :], tm=512)


def kernel(x, norm_pre_w, w_in, mlstm_i_bias, mlstm_f_bias, qk_conv_w, qk_conv_b, mlstm_norm_w,
           pool_w, pool_scale, w_proj_mlstm, w_proj_pool, w_out, norm_post_w):
    bsz, seq, d = x.shape
    assert d == D_MODEL and w_in.shape[-1] == MAIN_COLS + GATE_COLS
    x2 = x.reshape(bsz * seq, d)
    for l in range(norm_pre_w.shape[0]):
        x2 = _layer(x2, bsz, seq, l, norm_pre_w[l], w_in, mlstm_i_bias[l], mlstm_f_bias[l],
                    qk_conv_w[l], qk_conv_b[l], mlstm_norm_w[l], pool_w[l], pool_scale[l],
                    w_proj_mlstm[l], w_proj_pool[l], w_out[l], norm_post_w[l])
    return x2.reshape(bsz, seq, d)
```

```python
import functools

import jax
import jax.numpy as jnp
from jax import lax
from jax.experimental import pallas as pl
from jax.experimental.pallas import tpu as pltpu

D_MODEL = 2048
HEADS = 8
HEAD_DIM = D_MODEL // HEADS
CONV_WIDTH = 4
POOL_GROUPS = 4
POOL_WIDTH = D_MODEL // 2
POOL_GROUP_DIM = POOL_WIDTH // POOL_GROUPS
POOL_WINDOWS = (2, 4, 8, 16)
RMS_EPS = 1e-6
GATE_COLS = 2 * HEADS
OFF_GATES = 5 * D_MODEL
OFF_POOL = OFF_GATES + GATE_COLS
MAIN_COLS = 5 * D_MODEL + 2 * POOL_WIDTH + 2 * D_MODEL

SUBLANES = 8
LANES = 128
MXU_COLS = 256
HALO_POOL = 2 * SUBLANES
assert all(w & (w - 1) == 0 and w <= HALO_POOL for w in POOL_WINDOWS)
VMEM_LIMIT = 56 * 1024 * 1024
HEAD_SKEW = 1

F32 = jnp.float32
BF16 = jnp.bfloat16


def _sigmoid(x):
    return 0.5 * jnp.tanh(0.5 * x) + 0.5


def _silu(x):
    half = 0.5 * x
    return half * jnp.tanh(half) + half


def _log_sigmoid(x):
    return jnp.minimum(x, 0.0) - jnp.log1p(jnp.exp(-jnp.abs(x)))


def _chunk_scan(x, combine, fill, lane_in_chunk, chunk):
    sh = 1
    while sh < chunk:
        x = combine(x, jnp.where(lane_in_chunk >= sh, pltpu.roll(x, sh, 1), fill))
        sh *= 2
    return x


def _prenorm_kernel(x_ref, nw_ref, wgt_ref, bias_ref, h_ref, gates_ref, *, chunk):
    x = x_ref[...]
    y = x * lax.rsqrt(jnp.mean(x * x, axis=-1, keepdims=True) + RMS_EPS) * nw_ref[...]
    hb = y.astype(BF16)
    h_ref[...] = hb
    gr = lax.dot_general(wgt_ref[...].astype(BF16), hb, (((1,), (1,)), ((), ())),
                         preferred_element_type=F32) + bias_ref[...]
    log_i = gr[0:HEADS, :]
    log_f = _log_sigmoid(gr[HEADS:GATE_COLS, :])
    lane_in_chunk = lax.broadcasted_iota(jnp.int32, log_i.shape, 1) % chunk
    b = _chunk_scan(log_f, jnp.add, 0.0, lane_in_chunk, chunk)
    a = log_i - b
    a_max = _chunk_scan(a, jnp.maximum, -jnp.inf, lane_in_chunk, chunk)
    gates_ref[...] = jnp.concatenate([b, a, a_max], axis=0)


def _prenorm(x2, norm_w, w_in_t, layer, bias_col, tm, chunk):
    t = x2.shape[0]
    return pl.pallas_call(
        functools.partial(_prenorm_kernel, chunk=chunk),
        grid=(t // tm,),
        in_specs=[
            pl.BlockSpec((tm, D_MODEL), lambda i: (i, 0)),
            pl.BlockSpec((1, D_MODEL), lambda i: (0, 0)),
            pl.BlockSpec((None, GATE_COLS, D_MODEL), lambda i: (layer, OFF_GATES // GATE_COLS, 0)),
            pl.BlockSpec((GATE_COLS, 1), lambda i: (0, 0)),
        ],
        out_specs=[
            pl.BlockSpec((tm, D_MODEL), lambda i: (i, 0)),
            pl.BlockSpec((3 * HEADS, tm), lambda i: (0, i)),
        ],
        out_shape=[
            jax.ShapeDtypeStruct((t, D_MODEL), BF16),
            jax.ShapeDtypeStruct((3 * HEADS, t), F32),
        ],
        compiler_params=pltpu.CompilerParams(
            dimension_semantics=("parallel",), vmem_limit_bytes=VMEM_LIMIT),
        name="prenorm",
    )(x2, norm_w, w_in_t, bias_col)


def _proj_kernel(h_ref, w_ref, wn_ref, cw_ref, cb_ref, o_ref, wb_ref, stage_ref,
                 *, tm, tn, tiles_per_seq):
    j = pl.program_id(0)
    i = pl.program_id(1)
    seg = D_MODEL // tn
    shift_tile = OFF_GATES // tn
    nrows = 256

    @pl.when(jnp.logical_and(i == 0, j < shift_tile))
    def _():
        for r in range(0, tn, nrows):
            wb_ref[r:r + nrows, :] = w_ref[r:r + nrows, :].astype(BF16)

    @pl.when(jnp.logical_and(i == 0, j >= shift_tile))
    def _():
        for r in range(0, tn - nrows, nrows):
            wb_ref[r:r + nrows, :] = w_ref[r + GATE_COLS:r + GATE_COLS + nrows, :].astype(BF16)
        r = tn - nrows
        wb_ref[r:tn - GATE_COLS, :] = w_ref[r + GATE_COLS:tn, :].astype(BF16)
        wb_ref[tn - GATE_COLS:tn, :] = wn_ref[...].astype(BF16)

    def matmul(c):
        return lax.dot_general(h_ref[...], wb_ref[c:c + MXU_COLS, :], (((1,), (1,)), ((), ())),
                               preferred_element_type=F32)

    def stripes(epilogue):
        for c in range(0, tn, MXU_COLS):
            o_ref[:, c:c + MXU_COLS] = epilogue(matmul(c)).astype(BF16)

    is_conv = j < 2 * seg
    pool_tile = shift_tile
    is_plain = jnp.logical_or(jnp.logical_and(j >= 2 * seg, j < 3 * seg), j == pool_tile)
    is_silu = jnp.logical_or(jnp.logical_and(j >= 4 * seg, j < 5 * seg),
                             jnp.logical_and(j > pool_tile, j < pool_tile + 2 * (POOL_WIDTH // tn)))
    is_sigmoid = jnp.logical_not(jnp.logical_or(jnp.logical_or(is_conv, is_plain), is_silu))

    @pl.when(jnp.logical_and(i == 0, j == 0))
    def _():
        stage_ref[...] = jnp.zeros_like(stage_ref)

    def conv_tile(out_scale):
        seq_start = i % tiles_per_seq == 0
        groups = tm // SUBLANES + 1
        sublane = lax.broadcasted_iota(jnp.int32, (groups, SUBLANES, MXU_COLS), 1)

        def shift_down(a, n):
            rot = pltpu.roll(a, n, 1)
            above = jnp.concatenate([rot[groups - 1:], rot[:groups - 1]], axis=0)
            return jnp.where(sublane < n, above, rot)

        def stage(s):
            history = stage_ref[s, tm:tm + SUBLANES, :]
            stage_ref[s, 0:SUBLANES, :] = jnp.where(seq_start, 0.0, history)
            stage_ref[s, SUBLANES:, :] = matmul(s * MXU_COLS)

        def finish(s):
            c = s * MXU_COLS
            w = [cw_ref[k:k + 1, c:c + MXU_COLS] for k in range(CONV_WIDTH)]
            ext = stage_ref[s].reshape(groups, SUBLANES, MXU_COLS)
            ext1 = shift_down(ext, 1)
            acc = (w[3] * ext + w[2] * ext1) + shift_down(w[1] * ext + w[0] * ext1, 2)
            y = acc[1:].reshape(tm, MXU_COLS) + cb_ref[:, c:c + MXU_COLS]
            act = _silu(y.astype(BF16))
            o_ref[:, c:c + MXU_COLS] = act if out_scale is None else act * out_scale

        n_stripes = tn // MXU_COLS
        stage(0)
        for s in range(1, n_stripes):
            stage(s)
            finish(s - 1)
        finish(n_stripes - 1)

    @pl.when(j < seg)
    def _():
        conv_tile(HEAD_DIM ** -0.5)

    @pl.when(jnp.logical_and(j >= seg, j < 2 * seg))
    def _():
        conv_tile(None)

    @pl.when(is_plain)
    def _():
        stripes(lambda r: r)

    @pl.when(is_silu)
    def _():
        stripes(lambda r: _silu(r.astype(BF16)))

    @pl.when(is_sigmoid)
    def _():
        stripes(lambda r: _sigmoid(r.astype(BF16)))


def _proj(h, w_in_t, layer, conv_w, conv_b, seq, tm, tn):
    t = h.shape[0]
    n_tiles = MAIN_COLS // tn
    shift_tile = OFF_GATES // tn
    conv_tiles = 2 * D_MODEL // tn
    gate_blocks_per_tile = tn // GATE_COLS
    conv_block = lambda j, i: (0, jnp.minimum(j, conv_tiles - 1))
    return pl.pallas_call(
        functools.partial(_proj_kernel, tm=tm, tn=tn, tiles_per_seq=seq // tm),
        grid=(n_tiles, t // tm),
        in_specs=[
            pl.BlockSpec((tm, D_MODEL), lambda j, i: (i, 0)),
            pl.BlockSpec((None, tn, D_MODEL), lambda j, i: (layer, j, 0)),
            pl.BlockSpec((None, GATE_COLS, D_MODEL),
                         lambda j, i: (layer, jnp.where(j >= shift_tile, (j + 1) * gate_blocks_per_tile, 0), 0)),
            pl.BlockSpec((CONV_WIDTH, tn), conv_block),
            pl.BlockSpec((1, tn), conv_block),
        ],
        out_specs=pl.BlockSpec((tm, tn), lambda j, i: (i, j)),
        out_shape=jax.ShapeDtypeStruct((t, MAIN_COLS), BF16),
        scratch_shapes=[pltpu.VMEM((tn, D_MODEL), BF16),
                        pltpu.VMEM((tn // MXU_COLS, tm + SUBLANES, MXU_COLS), F32)],
        compiler_params=pltpu.CompilerParams(
            dimension_semantics=("arbitrary", "arbitrary"), vmem_limit_bytes=VMEM_LIMIT),
        name="proj",
    )(h, w_in_t, w_in_t, conv_w, conv_b)


def _mlstm_kernel(q_ref, k_ref, v_ref, o_ref, z_ref, gates_ref, nw_ref, y_ref,
                  caug_ref, m_ref, *, chunk):
    L = chunk
    d = HEAD_DIM

    @pl.when(pl.program_id(1) == 0)
    def _():
        caug_ref[...] = jnp.zeros_like(caug_ref)
        m_ref[...] = jnp.zeros_like(m_ref)

    b_row = gates_ref[0:HEADS, :]
    a_row = gates_ref[HEADS:2 * HEADS, :]
    a_max = gates_ref[2 * HEADS:3 * HEADS, :]
    m_st = m_ref[...]
    big_m = jnp.maximum(a_max, m_st)
    g = b_row[:, L - 1:L]
    m_new = jnp.maximum(g + m_st, g + a_max[:, L - 1:L])
    decay = jnp.exp(g + m_st - m_new)
    w_inter = jnp.exp(m_st - big_m)
    floor = jnp.exp(-(b_row + big_m))
    w_new = jnp.exp(g + a_row - m_new)
    m_ref[...] = m_new

    rows = jnp.concatenate(
        [big_m, w_inter, floor, w_new, jnp.zeros((LANES - 4 * HEADS, L), F32)], axis=0)
    cols = rows.T

    def column(j):
        return jnp.broadcast_to(cols[:, j:j + 1], (L, LANES))

    def twice(x):
        return jnp.concatenate([x, x], axis=1)

    t_idx = lax.broadcasted_iota(jnp.int32, (L, L), 0)
    s_idx = lax.broadcasted_iota(jnp.int32, (L, L), 1)
    causal = s_idx <= t_idx
    ones_blk = jnp.ones((L, LANES), BF16)
    ones_sq = jnp.ones((d, LANES), BF16)

    def head_stages(hh):
        col = hh * d
        v = {}

        def load_and_score():
            v["qb"] = q_ref[:, col:col + d]
            v["kb"] = k_ref[:, col:col + d]
            v["vaug"] = jnp.concatenate([v_ref[:, col:col + d], ones_blk], axis=1)
            v["caug"] = caug_ref[hh]
            v["s"] = lax.dot_general(v["qb"], v["kb"], (((1,), (1,)), ((), ())),
                                     preferred_element_type=F32)
            v["qc"] = jnp.dot(v["qb"], v["caug"].astype(BF16),
                              preferred_element_type=F32)

        def intra():
            log_d = jnp.where(causal, a_row[hh:hh + 1, :] - twice(column(hh)), -jnp.inf)
            p = jnp.exp(log_d) * v.pop("s")
            v["pv"] = jnp.dot(p.astype(BF16), v["vaug"], preferred_element_type=F32)

        def combine():
            w_inter_c = column(HEADS + hh)
            pv, qc = v.pop("pv"), v.pop("qc")
            num = pv[:, :d] + twice(w_inter_c) * qc[:, :d]
            den = pv[:, d:] + w_inter_c * qc[:, d:]
            v["inv"] = 1.0 / jnp.maximum(jnp.abs(den), column(2 * HEADS + hh))
            v["ssq"] = jnp.dot((num * num).astype(BF16), ones_sq, preferred_element_type=F32)
            v["num"] = num

        def emit():
            inv = v.pop("inv")
            r = inv * lax.rsqrt(v.pop("ssq") * (1.0 / d) * (inv * inv) + RMS_EPS)
            hn = v.pop("num") * twice(r) * nw_ref[:, col:col + d]
            og = o_ref[:, col:col + d].astype(F32)
            za = z_ref[:, col:col + d].astype(F32)
            y_ref[:, col:col + d] = (og * hn * za).astype(BF16)

        def carry():
            wk = (twice(column(3 * HEADS + hh)) * v.pop("kb").astype(F32)).astype(BF16)
            upd = lax.dot_general(wk, v.pop("vaug"), (((0,), (0,)), ((), ())),
                                  preferred_element_type=F32)
            caug_ref[hh] = decay[hh:hh + 1, :] * v.pop("caug") + upd

        return (load_and_score, intra, combine, emit, carry)

    schedule = sorted((hh * HEAD_SKEW + k, hh, k) for hh in range(HEADS) for k in range(5))
    all_stages = [head_stages(hh) for hh in range(HEADS)]
    for _, hh, k in schedule:
        all_stages[hh][k]()


def _mlstm(proj, gates, norm_w, bsz, seq, chunk):
    nc = seq // chunk
    t = bsz * seq
    col_block = lambda cb: pl.BlockSpec((chunk, D_MODEL), lambda b, c: (b * nc + c, cb))
    const = lambda shape: pl.BlockSpec(shape, lambda b, c: (0,) * len(shape))
    return pl.pallas_call(
        functools.partial(_mlstm_kernel, chunk=chunk),
        grid=(bsz, nc),
        in_specs=[
            col_block(0), col_block(1), col_block(2), col_block(3), col_block(4),
            pl.BlockSpec((3 * HEADS, chunk), lambda b, c: (0, b * nc + c)),
            const((1, D_MODEL)),
        ],
        out_specs=pl.BlockSpec((chunk, D_MODEL), lambda b, c: (b * nc + c, 0)),
        out_shape=jax.ShapeDtypeStruct((t, D_MODEL), BF16),
        scratch_shapes=[
            pltpu.VMEM((HEADS, HEAD_DIM, HEAD_DIM + LANES), F32),
            pltpu.VMEM((HEADS, 1), F32),
        ],
        compiler_params=pltpu.CompilerParams(
            dimension_semantics=("parallel", "arbitrary"), vmem_limit_bytes=VMEM_LIMIT),
        name="mlstm",
    )(proj, proj, proj, proj, proj, gates, norm_w)


def _pool_kernel(u_ref, z_ref, pw_ref, ps_ref, y_ref, ext_ref, *, rows):
    c = pl.program_id(1)

    @pl.when(c == 0)
    def _():
        ext_ref[0:HALO_POOL, :] = jnp.zeros((HALO_POOL, POOL_WIDTH), F32)

    ext_ref[HALO_POOL:HALO_POOL + rows, :] = u_ref[...].astype(F32)
    pos = lax.broadcasted_iota(jnp.int32, (rows, 1), 0) + c * rows + 1
    history_groups = HALO_POOL // SUBLANES
    groups = rows // SUBLANES + history_groups
    sublane = lax.broadcasted_iota(jnp.int32, (groups, SUBLANES, POOL_GROUP_DIM), 1)

    def shift_down(a, j):
        if j % SUBLANES == 0:
            k = j // SUBLANES
            return jnp.concatenate([a[groups - k:], a[:groups - k]], axis=0)
        rot = pltpu.roll(a, j, 1)
        above = jnp.concatenate([rot[groups - 1:], rot[:groups - 1]], axis=0)
        return jnp.where(sublane < j, above, rot)

    for g in range(POOL_GROUPS):
        col = g * POOL_GROUP_DIM
        win = POOL_WINDOWS[g]
        ext = ext_ref[:, col:col + POOL_GROUP_DIM].reshape(groups, SUBLANES, POOL_GROUP_DIM)
        wsum, width = ext, 1
        while width < win:
            wsum = wsum + shift_down(wsum, width)
            width *= 2
        u = ext[history_groups:].reshape(rows, POOL_GROUP_DIM)
        wsum = wsum[history_groups:].reshape(rows, POOL_GROUP_DIM)
        count = jnp.minimum(pos, win).astype(F32)
        pooled = wsum / count - u
        mixed = jnp.dot(pooled.astype(BF16), pw_ref[g], preferred_element_type=F32)
        mixed = mixed * ps_ref[:, col:col + POOL_GROUP_DIM]
        zb = z_ref[:, col:col + POOL_GROUP_DIM].astype(F32)
        y_ref[:, col:col + POOL_GROUP_DIM] = (mixed * zb).astype(BF16)

    ext_ref[0:HALO_POOL, :] = ext_ref[rows:rows + HALO_POOL, :]


def _pool(proj, pool_w, pool_scale, bsz, seq, rows):
    nc = seq // rows
    t = bsz * seq
    u_block = OFF_GATES // POOL_WIDTH
    return pl.pallas_call(
        functools.partial(_pool_kernel, rows=rows),
        grid=(bsz, nc),
        in_specs=[
            pl.BlockSpec((rows, POOL_WIDTH), lambda b, c: (b * nc + c, u_block)),
            pl.BlockSpec((rows, POOL_WIDTH), lambda b, c: (b * nc + c, u_block + 1)),
            pl.BlockSpec((POOL_GROUPS, POOL_GROUP_DIM, POOL_GROUP_DIM), lambda b, c: (0, 0, 0)),
            pl.BlockSpec((1, POOL_WIDTH), lambda b, c: (0, 0)),
        ],
        out_specs=pl.BlockSpec((rows, POOL_WIDTH), lambda b, c: (b * nc + c, 0)),
        out_shape=jax.ShapeDtypeStruct((t, POOL_WIDTH), BF16),
        scratch_shapes=[pltpu.VMEM((rows + HALO_POOL, POOL_WIDTH), F32)],
        compiler_params=pltpu.CompilerParams(
            dimension_semantics=("parallel", "arbitrary"), vmem_limit_bytes=VMEM_LIMIT),
        name="pool",
    )(proj, proj, pool_w, pool_scale)


def _merge_kernel(ya_ref, yb_ref, ga_ref, gb_ref, wa_ref, wb_ref, o_ref):
    br_a = jnp.dot(ya_ref[...], wa_ref[...], preferred_element_type=F32)
    br_b = jnp.dot(yb_ref[...], wb_ref[...], preferred_element_type=F32)
    merged = ga_ref[...].astype(F32) * br_a + gb_ref[...].astype(F32) * br_b
    o_ref[...] = merged.astype(BF16)


def _merge(y_a, y_b, proj, w_a, w_b, tm):
    t = y_a.shape[0]
    ga_block = (OFF_GATES + 2 * POOL_WIDTH) // D_MODEL
    return pl.pallas_call(
        _merge_kernel,
        grid=(t // tm,),
        in_specs=[
            pl.BlockSpec((tm, D_MODEL), lambda i: (i, 0)),
            pl.BlockSpec((tm, POOL_WIDTH), lambda i: (i, 0)),
            pl.BlockSpec((tm, D_MODEL), lambda i: (i, ga_block)),
            pl.BlockSpec((tm, D_MODEL), lambda i: (i, ga_block + 1)),
            pl.BlockSpec((D_MODEL, D_MODEL), lambda i: (0, 0)),
            pl.BlockSpec((POOL_WIDTH, D_MODEL), lambda i: (0, 0)),
        ],
        out_specs=pl.BlockSpec((tm, D_MODEL), lambda i: (i, 0)),
        out_shape=jax.ShapeDtypeStruct((t, D_MODEL), BF16),
        compiler_params=pltpu.CompilerParams(
            dimension_semantics=("parallel",), vmem_limit_bytes=VMEM_LIMIT),
        name="merge",
    )(y_a, y_b, proj, proj, w_a, w_b)


def _outproj_kernel(m_ref, w_ref, x_ref, nw_ref, o_ref):
    out = jnp.dot(m_ref[...], w_ref[...], preferred_element_type=F32)
    y = out * lax.rsqrt(jnp.mean(out * out, axis=-1, keepdims=True) + RMS_EPS) * nw_ref[...]
    o_ref[...] = x_ref[...] + y


def _outproj(merged, w_out, x2, norm_w, tm):
    t = merged.shape[0]
    return pl.pallas_call(
        _outproj_kernel,
        grid=(t // tm,),
        in_specs=[
            pl.BlockSpec((tm, D_MODEL), lambda i: (i, 0)),
            pl.BlockSpec((D_MODEL, D_MODEL), lambda i: (0, 0)),
            pl.BlockSpec((tm, D_MODEL), lambda i: (i, 0)),
            pl.BlockSpec((1, D_MODEL), lambda i: (0, 0)),
        ],
        out_specs=pl.BlockSpec((tm, D_MODEL), lambda i: (i, 0)),
        out_shape=jax.ShapeDtypeStruct((t, D_MODEL), F32),
        compiler_params=pltpu.CompilerParams(
            dimension_semantics=("parallel",), vmem_limit_bytes=VMEM_LIMIT),
        name="outproj",
    )(merged, w_out, x2, norm_w)


def _layer(x2, bsz, seq, layer, norm_pre_w, w_in, i_bias, f_bias, conv_w, conv_b, mlstm_norm_w,
           pool_w, pool_scale, w_proj_mlstm, w_proj_pool, w_out, norm_post_w):
    w_in_t = jnp.swapaxes(w_in, 1, 2)
    bias = jnp.concatenate([i_bias, f_bias]).astype(F32)

    chunk = 256
    h, gates = _prenorm(x2, norm_pre_w[None, :], w_in_t, layer, bias[:, None], tm=1024, chunk=chunk)
    proj = _proj(h, w_in_t, layer, conv_w, conv_b[None, :], seq, tm=1024, tn=1024)
    y_a = _mlstm(proj, gates, mlstm_norm_w[None, :], bsz, seq, chunk=chunk)
    y_b = _pool(proj, pool_w.astype(BF16), pool_scale[None, :], bsz, seq, rows=512)
    merged = _merge(y_a, y_b, proj, w_proj_mlstm.astype(BF16), w_proj_pool.astype(BF16), tm=512)
    return _outproj(merged, w_out.astype(BF16), x2, norm_post_w[None, :], tm=512)


def kernel(x, norm_pre_w, w_in, mlstm_i_bias, mlstm_f_bias, qk_conv_w, qk_conv_b, mlstm_norm_w,
           pool_w, pool_scale, w_proj_mlstm, w_proj_pool, w_out, norm_post_w):
    bsz, seq, d = x.shape
    assert d == D_MODEL and w_in.shape[-1] == MAIN_COLS + GATE_COLS
    x2 = x.reshape(bsz * seq, d)
    for l in range(norm_pre_w.shape[0]):
        x2 = _layer(x2, bsz, seq, l, norm_pre_w[l], w_in, mlstm_i_bias[l], mlstm_f_bias[l],
                    qk_conv_w[l], qk_conv_b[l], mlstm_norm_w[l], pool_w[l], pool_scale[l],
                    w_proj_mlstm[l], w_proj_pool[l], w_out[l], norm_post_w[l])
    return x2.reshape(bsz, seq, d)
```

```python
import functools

import jax
import jax.numpy as jnp
from jax import lax
from jax.experimental import pallas as pl
from jax.experimental.pallas import tpu as pltpu

D_MODEL = 2048
HEADS = 8
HEAD_DIM = D_MODEL // HEADS
CONV_WIDTH = 4
POOL_GROUPS = 4
POOL_WIDTH = D_MODEL // 2
POOL_GROUP_DIM = POOL_WIDTH // POOL_GROUPS
POOL_WINDOWS = (2, 4, 8, 16)
RMS_EPS = 1e-6
GATE_COLS = 2 * HEADS
OFF_GATES = 5 * D_MODEL
OFF_POOL = OFF_GATES + GATE_COLS
MAIN_COLS = 5 * D_MODEL + 2 * POOL_WIDTH + 2 * D_MODEL

SUBLANES = 8
LANES = 128
MXU_COLS = 256
HALO_POOL = 2 * SUBLANES
assert all(w & (w - 1) == 0 and w <= HALO_POOL for w in POOL_WINDOWS)
VMEM_LIMIT = 56 * 1024 * 1024
HEAD_SKEW = 1

F32 = jnp.float32
BF16 = jnp.bfloat16


def _sigmoid(x):
    return 0.5 * jnp.tanh(0.5 * x) + 0.5


def _silu(x):
    half = 0.5 * x
    return half * jnp.tanh(half) + half


def _log_sigmoid(x):
    return jnp.minimum(x, 0.0) - jnp.log1p(jnp.exp(-jnp.abs(x)))


def _chunk_scan(x, combine, fill, lane_in_chunk, chunk):
    sh = 1
    while sh < chunk:
        x = combine(x, jnp.where(lane_in_chunk >= sh, pltpu.roll(x, sh, 1), fill))
        sh *= 2
    return x


def _prenorm_kernel(x_ref, nw_ref, wgt_ref, bias_ref, h_ref, gates_ref, *, chunk):
    x = x_ref[...]
    y = x * lax.rsqrt(jnp.mean(x * x, axis=-1, keepdims=True) + RMS_EPS) * nw_ref[...]
    hb = y.astype(BF16)
    h_ref[...] = hb
    gr = lax.dot_general(wgt_ref[...].astype(BF16), hb, (((1,), (1,)), ((), ())),
                         preferred_element_type=F32) + bias_ref[...]
    log_i = gr[0:HEADS, :]
    log_f = _log_sigmoid(gr[HEADS:GATE_COLS, :])
    lane_in_chunk = lax.broadcasted_iota(jnp.int32, log_i.shape, 1) % chunk
    b = _chunk_scan(log_f, jnp.add, 0.0, lane_in_chunk, chunk)
    a = log_i - b
    a_max = _chunk_scan(a, jnp.maximum, -jnp.inf, lane_in_chunk, chunk)
    gates_ref[...] = jnp.concatenate([b, a, a_max], axis=0)


def _prenorm(x2, norm_w, w_in_t, layer, bias_col, tm, chunk):
    t = x2.shape[0]
    return pl.pallas_call(
        functools.partial(_prenorm_kernel, chunk=chunk),
        grid=(t // tm,),
        in_specs=[
            pl.BlockSpec((tm, D_MODEL), lambda i: (i, 0)),
            pl.BlockSpec((1, D_MODEL), lambda i: (0, 0)),
            pl.BlockSpec((None, GATE_COLS, D_MODEL), lambda i: (layer, OFF_GATES // GATE_COLS, 0)),
            pl.BlockSpec((GATE_COLS, 1), lambda i: (0, 0)),
        ],
        out_specs=[
            pl.BlockSpec((tm, D_MODEL), lambda i: (i, 0)),
            pl.BlockSpec((3 * HEADS, tm), lambda i: (0, i)),
        ],
        out_shape=[
            jax.ShapeDtypeStruct((t, D_MODEL), BF16),
            jax.ShapeDtypeStruct((3 * HEADS, t), F32),
        ],
        compiler_params=pltpu.CompilerParams(
            dimension_semantics=("parallel",), vmem_limit_bytes=VMEM_LIMIT),
        name="prenorm",
    )(x2, norm_w, w_in_t, bias_col)


def _proj_kernel(h_ref, w_ref, wn_ref, cw_ref, cb_ref, o_ref, wb_ref, stage_ref,
                 *, tm, tn, tiles_per_seq):
    j = pl.program_id(0)
    i = pl.program_id(1)
    seg = D_MODEL // tn
    shift_tile = OFF_GATES // tn
    nrows = 256

    @pl.when(jnp.logical_and(i == 0, j < shift_tile))
    def _():
        for r in range(0, tn, nrows):
            wb_ref[r:r + nrows, :] = w_ref[r:r + nrows, :].astype(BF16)

    @pl.when(jnp.logical_and(i == 0, j >= shift_tile))
    def _():
        for r in range(0, tn - nrows, nrows):
            wb_ref[r:r + nrows, :] = w_ref[r + GATE_COLS:r + GATE_COLS + nrows, :].astype(BF16)
        r = tn - nrows
        wb_ref[r:tn - GATE_COLS, :] = w_ref[r + GATE_COLS:tn, :].astype(BF16)
        wb_ref[tn - GATE_COLS:tn, :] = wn_ref[...].astype(BF16)

    def matmul(c):
        return lax.dot_general(h_ref[...], wb_ref[c:c + MXU_COLS, :], (((1,), (1,)), ((), ())),
                               preferred_element_type=F32)

    def stripes(epilogue):
        for c in range(0, tn, MXU_COLS):
            o_ref[:, c:c + MXU_COLS] = epilogue(matmul(c)).astype(BF16)

    is_conv = j < 2 * seg
    pool_tile = shift_tile
    is_plain = jnp.logical_or(jnp.logical_and(j >= 2 * seg, j < 3 * seg), j == pool_tile)
    is_silu = jnp.logical_or(jnp.logical_and(j >= 4 * seg, j < 5 * seg),
                             jnp.logical_and(j > pool_tile, j < pool_tile + 2 * (POOL_WIDTH // tn)))
    is_sigmoid = jnp.logical_not(jnp.logical_or(jnp.logical_or(is_conv, is_plain), is_silu))

    @pl.when(jnp.logical_and(i == 0, j == 0))
    def _():
        stage_ref[...] = jnp.zeros_like(stage_ref)

    def conv_tile(out_scale):
        seq_start = i % tiles_per_seq == 0
        groups = tm // SUBLANES + 1
        sublane = lax.broadcasted_iota(jnp.int32, (groups, SUBLANES, MXU_COLS), 1)

        def shift_down(a, n):
            rot = pltpu.roll(a, n, 1)
            above = jnp.concatenate([rot[groups - 1:], rot[:groups - 1]], axis=0)
            return jnp.where(sublane < n, above, rot)

        def stage(s):
            history = stage_ref[s, tm:tm + SUBLANES, :]
            stage_ref[s, 0:SUBLANES, :] = jnp.where(seq_start, 0.0, history)
            stage_ref[s, SUBLANES:, :] = matmul(s * MXU_COLS)

        def finish(s):
            c = s * MXU_COLS
            w = [cw_ref[k:k + 1, c:c + MXU_COLS] for k in range(CONV_WIDTH)]
            ext = stage_ref[s].reshape(groups, SUBLANES, MXU_COLS)
            ext1 = shift_down(ext, 1)
            acc = (w[3] * ext + w[2] * ext1) + shift_down(w[1] * ext + w[0] * ext1, 2)
            y = acc[1:].reshape(tm, MXU_COLS) + cb_ref[:, c:c + MXU_COLS]
            act = _silu(y.astype(BF16))
            o_ref[:, c:c + MXU_COLS] = act if out_scale is None else act * out_scale

        n_stripes = tn // MXU_COLS
        stage(0)
        for s in range(1, n_stripes):
            stage(s)
            finish(s - 1)
        finish(n_stripes - 1)

    @pl.when(j < seg)
    def _():
        conv_tile(HEAD_DIM ** -0.5)

    @pl.when(jnp.logical_and(j >= seg, j < 2 * seg))
    def _():
        conv_tile(None)

    @pl.when(is_plain)
    def _():
        stripes(lambda r: r)

    @pl.when(is_silu)
    def _():
        stripes(lambda r: _silu(r.astype(BF16)))

    @pl.when(is_sigmoid)
    def _():
        stripes(lambda r: _sigmoid(r.astype(BF16)))


def _proj(h, w_in_t, layer, conv_w, conv_b, seq, tm, tn):
    t = h.shape[0]
    n_tiles = MAIN_COLS // tn
    shift_tile = OFF_GATES // tn
    conv_tiles = 2 * D_MODEL // tn
    gate_blocks_per_tile = tn // GATE_COLS
    conv_block = lambda j, i: (0, jnp.minimum(j, conv_tiles - 1))
    return pl.pallas_call(
        functools.partial(_proj_kernel, tm=tm, tn=tn, tiles_per_seq=seq // tm),
        grid=(n_tiles, t // tm),
        in_specs=[
            pl.BlockSpec((tm, D_MODEL), lambda j, i: (i, 0)),
            pl.BlockSpec((None, tn, D_MODEL), lambda j, i: (layer, j, 0)),
            pl.BlockSpec((None, GATE_COLS, D_MODEL),
                         lambda j, i: (layer, jnp.where(j >= shift_tile, (j + 1) * gate_blocks_per_tile, 0), 0)),
            pl.BlockSpec((CONV_WIDTH, tn), conv_block),
            pl.BlockSpec((1, tn), conv_block),
        ],
        out_specs=pl.BlockSpec((tm, tn), lambda j, i: (i, j)),
        out_shape=jax.ShapeDtypeStruct((t, MAIN_COLS), BF16),
        scratch_shapes=[pltpu.VMEM((tn, D_MODEL), BF16),
                        pltpu.VMEM((tn // MXU_COLS, tm + SUBLANES, MXU_COLS), F32)],
        compiler_params=pltpu.CompilerParams(
            dimension_semantics=("arbitrary", "arbitrary"), vmem_limit_bytes=VMEM_LIMIT),
        name="proj",
    )(h, w_in_t, w_in_t, conv_w, conv_b)


def _mlstm_kernel(q_ref, k_ref, v_ref, o_ref, z_ref, gates_ref, nw_ref, *rest, chunk, n_cast):
    L = chunk
    d = HEAD_DIM
    cast_in, (y_ref, *cast_out), (caug_ref, m_ref) = (
        rest[:n_cast], rest[n_cast:2 * n_cast + 1], rest[2 * n_cast + 1:])

    for src, dst in zip(cast_in, cast_out):
        dst[...] = src[...].astype(BF16)

    @pl.when(pl.program_id(1) == 0)
    def _():
        caug_ref[...] = jnp.zeros_like(caug_ref)
        m_ref[...] = jnp.zeros_like(m_ref)

    b_row = gates_ref[0:HEADS, :]
    a_row = gates_ref[HEADS:2 * HEADS, :]
    a_max = gates_ref[2 * HEADS:3 * HEADS, :]
    m_st = m_ref[...]
    big_m = jnp.maximum(a_max, m_st)
    g = b_row[:, L - 1:L]
    m_new = jnp.maximum(g + m_st, g + a_max[:, L - 1:L])
    decay = jnp.exp(g + m_st - m_new)
    w_inter = jnp.exp(m_st - big_m)
    floor = jnp.exp(-(b_row + big_m))
    w_new = jnp.exp(g + a_row - m_new)
    m_ref[...] = m_new

    rows = jnp.concatenate(
        [big_m, w_inter, floor, w_new, jnp.zeros((LANES - 4 * HEADS, L), F32)], axis=0)
    cols = rows.T

    def column(j):
        return jnp.broadcast_to(cols[:, j:j + 1], (L, LANES))

    def twice(x):
        return jnp.concatenate([x, x], axis=1)

    t_idx = lax.broadcasted_iota(jnp.int32, (L, L), 0)
    s_idx = lax.broadcasted_iota(jnp.int32, (L, L), 1)
    causal = s_idx <= t_idx
    ones_blk = jnp.ones((L, LANES), BF16)
    ones_sq = jnp.ones((d, LANES), BF16)

    def head_stages(hh):
        col = hh * d
        v = {}

        def load_and_score():
            v["qb"] = q_ref[:, col:col + d]
            v["kb"] = k_ref[:, col:col + d]
            v["vaug"] = jnp.concatenate([v_ref[:, col:col + d], ones_blk], axis=1)
            v["caug"] = caug_ref[hh]
            v["s"] = lax.dot_general(v["qb"], v["kb"], (((1,), (1,)), ((), ())),
                                     preferred_element_type=F32)
            v["qc"] = jnp.dot(v["qb"], v["caug"].astype(BF16),
                              preferred_element_type=F32)

        def intra():
            log_d = jnp.where(causal, a_row[hh:hh + 1, :] - twice(column(hh)), -jnp.inf)
            p = jnp.exp(log_d) * v.pop("s")
            v["pv"] = jnp.dot(p.astype(BF16), v["vaug"], preferred_element_type=F32)

        def combine():
            w_inter_c = column(HEADS + hh)
            pv, qc = v.pop("pv"), v.pop("qc")
            num = pv[:, :d] + twice(w_inter_c) * qc[:, :d]
            den = pv[:, d:] + w_inter_c * qc[:, d:]
            v["inv"] = 1.0 / jnp.maximum(jnp.abs(den), column(2 * HEADS + hh))
            v["ssq"] = jnp.dot((num * num).astype(BF16), ones_sq, preferred_element_type=F32)
            v["num"] = num

        def emit():
            inv = v.pop("inv")
            r = inv * lax.rsqrt(v.pop("ssq") * (1.0 / d) * (inv * inv) + RMS_EPS)
            hn = v.pop("num") * twice(r) * nw_ref[:, col:col + d]
            og = o_ref[:, col:col + d].astype(F32)
            za = z_ref[:, col:col + d].astype(F32)
            y_ref[:, col:col + d] = (og * hn * za).astype(BF16)

        def carry():
            wk = (twice(column(3 * HEADS + hh)) * v.pop("kb").astype(F32)).astype(BF16)
            upd = lax.dot_general(wk, v.pop("vaug"), (((0,), (0,)), ((), ())),
                                  preferred_element_type=F32)
            caug_ref[hh] = decay[hh:hh + 1, :] * v.pop("caug") + upd

        return (load_and_score, intra, combine, emit, carry)

    schedule = sorted((hh * HEAD_SKEW + k, hh, k) for hh in range(HEADS) for k in range(5))
    all_stages = [head_stages(hh) for hh in range(HEADS)]
    for _, hh, k in schedule:
        all_stages[hh][k]()


def _mlstm(proj, gates, norm_w, layer, weights, bsz, seq, chunk):
    nc = seq // chunk
    steps = bsz * nc
    t = bsz * seq
    col_block = lambda cb: pl.BlockSpec((chunk, D_MODEL), lambda b, c: (b * nc + c, cb))
    const = lambda shape: pl.BlockSpec(shape, lambda b, c: (0,) * len(shape))
    cast_in, cast_out, cast_shapes = [], [], []
    for w in weights:
        _, rows, cols = w.shape
        assert rows % (steps * 2 * SUBLANES) == 0
        cast_in.append(pl.BlockSpec((None, rows // steps, cols), lambda b, c: (layer, b * nc + c, 0)))
        cast_out.append(pl.BlockSpec((rows // steps, cols), lambda b, c: (b * nc + c, 0)))
        cast_shapes.append(jax.ShapeDtypeStruct((rows, cols), BF16))
    return pl.pallas_call(
        functools.partial(_mlstm_kernel, chunk=chunk, n_cast=len(weights)),
        grid=(bsz, nc),
        in_specs=[
            col_block(0), col_block(1), col_block(2), col_block(3), col_block(4),
            pl.BlockSpec((3 * HEADS, chunk), lambda b, c: (0, b * nc + c)),
            const((1, D_MODEL)),
        ] + cast_in,
        out_specs=[pl.BlockSpec((chunk, D_MODEL), lambda b, c: (b * nc + c, 0))] + cast_out,
        out_shape=[jax.ShapeDtypeStruct((t, D_MODEL), BF16)] + cast_shapes,
        scratch_shapes=[
            pltpu.VMEM((HEADS, HEAD_DIM, HEAD_DIM + LANES), F32),
            pltpu.VMEM((HEADS, 1), F32),
        ],
        compiler_params=pltpu.CompilerParams(
            dimension_semantics=("parallel", "arbitrary"), vmem_limit_bytes=VMEM_LIMIT),
        name="mlstm",
    )(proj, proj, proj, proj, proj, gates, norm_w, *weights)


def _pool_kernel(u_ref, z_ref, pw_ref, ps_ref, y_ref, ext_ref, *, rows):
    c = pl.program_id(1)

    @pl.when(c == 0)
    def _():
        ext_ref[0:HALO_POOL, :] = jnp.zeros((HALO_POOL, POOL_WIDTH), F32)

    ext_ref[HALO_POOL:HALO_POOL + rows, :] = u_ref[...].astype(F32)
    pos = lax.broadcasted_iota(jnp.int32, (rows, 1), 0) + c * rows + 1
    history_groups = HALO_POOL // SUBLANES
    groups = rows // SUBLANES + history_groups
    sublane = lax.broadcasted_iota(jnp.int32, (groups, SUBLANES, POOL_GROUP_DIM), 1)

    def shift_down(a, j):
        if j % SUBLANES == 0:
            k = j // SUBLANES
            return jnp.concatenate([a[groups - k:], a[:groups - k]], axis=0)
        rot = pltpu.roll(a, j, 1)
        above = jnp.concatenate([rot[groups - 1:], rot[:groups - 1]], axis=0)
        return jnp.where(sublane < j, above, rot)

    for g in range(POOL_GROUPS):
        col = g * POOL_GROUP_DIM
        win = POOL_WINDOWS[g]
        ext = ext_ref[:, col:col + POOL_GROUP_DIM].reshape(groups, SUBLANES, POOL_GROUP_DIM)
        wsum, width = ext, 1
        while width < win:
            wsum = wsum + shift_down(wsum, width)
            width *= 2
        u = ext[history_groups:].reshape(rows, POOL_GROUP_DIM)
        wsum = wsum[history_groups:].reshape(rows, POOL_GROUP_DIM)
        count = jnp.minimum(pos, win).astype(F32)
        pooled = wsum / count - u
        mixed = jnp.dot(pooled.astype(BF16), pw_ref[g], preferred_element_type=F32)
        mixed = mixed * ps_ref[:, col:col + POOL_GROUP_DIM]
        zb = z_ref[:, col:col + POOL_GROUP_DIM].astype(F32)
        y_ref[:, col:col + POOL_GROUP_DIM] = (mixed * zb).astype(BF16)

    ext_ref[0:HALO_POOL, :] = ext_ref[rows:rows + HALO_POOL, :]


def _pool(proj, pool_w, pool_scale, bsz, seq, rows):
    nc = seq // rows
    t = bsz * seq
    u_block = OFF_GATES // POOL_WIDTH
    return pl.pallas_call(
        functools.partial(_pool_kernel, rows=rows),
        grid=(bsz, nc),
        in_specs=[
            pl.BlockSpec((rows, POOL_WIDTH), lambda b, c: (b * nc + c, u_block)),
            pl.BlockSpec((rows, POOL_WIDTH), lambda b, c: (b * nc + c, u_block + 1)),
            pl.BlockSpec((POOL_GROUPS, POOL_GROUP_DIM, POOL_GROUP_DIM), lambda b, c: (0, 0, 0)),
            pl.BlockSpec((1, POOL_WIDTH), lambda b, c: (0, 0)),
        ],
        out_specs=pl.BlockSpec((rows, POOL_WIDTH), lambda b, c: (b * nc + c, 0)),
        out_shape=jax.ShapeDtypeStruct((t, POOL_WIDTH), BF16),
        scratch_shapes=[pltpu.VMEM((rows + HALO_POOL, POOL_WIDTH), F32)],
        compiler_params=pltpu.CompilerParams(
            dimension_semantics=("parallel", "arbitrary"), vmem_limit_bytes=VMEM_LIMIT),
        name="pool",
    )(proj, proj, pool_w, pool_scale)


def _merge_kernel(ya_ref, yb_ref, ga_ref, gb_ref, wa_ref, wb_ref, o_ref):
    br_a = jnp.dot(ya_ref[...], wa_ref[...], preferred_element_type=F32)
    br_b = jnp.dot(yb_ref[...], wb_ref[...], preferred_element_type=F32)
    merged = ga_ref[...].astype(F32) * br_a + gb_ref[...].astype(F32) * br_b
    o_ref[...] = merged.astype(BF16)


def _merge(y_a, y_b, proj, w_a, w_b, tm):
    t = y_a.shape[0]
    ga_block = (OFF_GATES + 2 * POOL_WIDTH) // D_MODEL
    return pl.pallas_call(
        _merge_kernel,
        grid=(t // tm,),
        in_specs=[
            pl.BlockSpec((tm, D_MODEL), lambda i: (i, 0)),
            pl.BlockSpec((tm, POOL_WIDTH), lambda i: (i, 0)),
            pl.BlockSpec((tm, D_MODEL), lambda i: (i, ga_block)),
            pl.BlockSpec((tm, D_MODEL), lambda i: (i, ga_block + 1)),
            pl.BlockSpec((D_MODEL, D_MODEL), lambda i: (0, 0)),
            pl.BlockSpec((POOL_WIDTH, D_MODEL), lambda i: (0, 0)),
        ],
        out_specs=pl.BlockSpec((tm, D_MODEL), lambda i: (i, 0)),
        out_shape=jax.ShapeDtypeStruct((t, D_MODEL), BF16),
        compiler_params=pltpu.CompilerParams(
            dimension_semantics=("parallel",), vmem_limit_bytes=VMEM_LIMIT),
        name="merge",
    )(y_a, y_b, proj, proj, w_a, w_b)


def _outproj_kernel(m_ref, w_ref, x_ref, nw_ref, o_ref):
    out = jnp.dot(m_ref[...], w_ref[...], preferred_element_type=F32)
    y = out * lax.rsqrt(jnp.mean(out * out, axis=-1, keepdims=True) + RMS_EPS) * nw_ref[...]
    o_ref[...] = x_ref[...] + y


def _outproj(merged, w_out, x2, norm_w, tm):
    t = merged.shape[0]
    return pl.pallas_call(
        _outproj_kernel,
        grid=(t // tm,),
        in_specs=[
            pl.BlockSpec((tm, D_MODEL), lambda i: (i, 0)),
            pl.BlockSpec((D_MODEL, D_MODEL), lambda i: (0, 0)),
            pl.BlockSpec((tm, D_MODEL), lambda i: (i, 0)),
            pl.BlockSpec((1, D_MODEL), lambda i: (0, 0)),
        ],
        out_specs=pl.BlockSpec((tm, D_MODEL), lambda i: (i, 0)),
        out_shape=jax.ShapeDtypeStruct((t, D_MODEL), F32),
        compiler_params=pltpu.CompilerParams(
            dimension_semantics=("parallel",), vmem_limit_bytes=VMEM_LIMIT),
        name="outproj",
    )(merged, w_out, x2, norm_w)


def _layer(x2, bsz, seq, layer, norm_pre_w, w_in, i_bias, f_bias, conv_w, conv_b, mlstm_norm_w,
           pool_w, pool_scale, w_proj_mlstm, w_proj_pool, w_out, norm_post_w):
    w_in_t = jnp.swapaxes(w_in, 1, 2)
    bias = jnp.concatenate([i_bias, f_bias]).astype(F32)

    chunk = 256
    h, gates = _prenorm(x2, norm_pre_w[None, :], w_in_t, layer, bias[:, None], tm=1024, chunk=chunk)
    proj = _proj(h, w_in_t, layer, conv_w, conv_b[None, :], seq, tm=1024, tn=1024)
    y_a, w_a, w_b, w_o = _mlstm(proj, gates, mlstm_norm_w[None, :], layer,
                                (w_proj_mlstm, w_proj_pool, w_out), bsz, seq, chunk=chunk)
    y_b = _pool(proj, pool_w.astype(BF16), pool_scale[None, :], bsz, seq, rows=512)
    merged = _merge(y_a, y_b, proj, w_a, w_b, tm=512)
    return _outproj(merged, w_o, x2, norm_post_w[None, :], tm=512)


def kernel(x, norm_pre_w, w_in, mlstm_i_bias, mlstm_f_bias, qk_conv_w, qk_conv_b, mlstm_norm_w,
           pool_w, pool_scale, w_proj_mlstm, w_proj_pool, w_out, norm_post_w):
    bsz, seq, d = x.shape
    assert d == D_MODEL and w_in.shape[-1] == MAIN_COLS + GATE_COLS
    x2 = x.reshape(bsz * seq, d)
    for l in range(norm_pre_w.shape[0]):
        x2 = _layer(x2, bsz, seq, l, norm_pre_w[l], w_in, mlstm_i_bias[l], mlstm_f_bias[l],
                    qk_conv_w[l], qk_conv_b[l], mlstm_norm_w[l], pool_w[l], pool_scale[l],
                    w_proj_mlstm, w_proj_pool, w_out, norm_post_w[l])
    return x2.reshape(bsz, seq, d)
```

```python
import functools

import jax
import jax.numpy as jnp
from jax import lax
from jax.experimental import pallas as pl
from jax.experimental.pallas import tpu as pltpu

D_MODEL = 2048
HEADS = 8
HEAD_DIM = D_MODEL // HEADS
CONV_WIDTH = 4
POOL_GROUPS = 4
POOL_WIDTH = D_MODEL // 2
POOL_GROUP_DIM = POOL_WIDTH // POOL_GROUPS
POOL_WINDOWS = (2, 4, 8, 16)
RMS_EPS = 1e-6
GATE_COLS = 2 * HEADS
OFF_GATES = 5 * D_MODEL
OFF_POOL = OFF_GATES + GATE_COLS
MAIN_COLS = 5 * D_MODEL + 2 * POOL_WIDTH + 2 * D_MODEL

SUBLANES = 8
LANES = 128
MXU_COLS = 256
HALO_POOL = 2 * SUBLANES
assert all(w & (w - 1) == 0 and w <= HALO_POOL for w in POOL_WINDOWS)
VMEM_LIMIT = 56 * 1024 * 1024
STAGE_ROWS = 512
CONV_ROWS = 64
HEAD_SKEW = 1

F32 = jnp.float32
BF16 = jnp.bfloat16


def _sigmoid(x):
    return 0.5 * jnp.tanh(0.5 * x) + 0.5


def _silu(x):
    half = 0.5 * x
    return half * jnp.tanh(half) + half


def _log_sigmoid(x):
    return jnp.minimum(x, 0.0) - jnp.log1p(jnp.exp(-jnp.abs(x)))


def _chunk_scan(x, combine, fill, lane_in_chunk, chunk):
    sh = 1
    while sh < chunk:
        x = combine(x, jnp.where(lane_in_chunk >= sh, pltpu.roll(x, sh, 1), fill))
        sh *= 2
    return x


def _prenorm_kernel(x_ref, nw_ref, wgt_ref, bias_ref, h_ref, gates_ref, *, chunk):
    x = x_ref[...]
    y = x * lax.rsqrt(jnp.mean(x * x, axis=-1, keepdims=True) + RMS_EPS) * nw_ref[...]
    hb = y.astype(BF16)
    h_ref[...] = hb
    gr = lax.dot_general(wgt_ref[...].astype(BF16), hb, (((1,), (1,)), ((), ())),
                         preferred_element_type=F32) + bias_ref[...]
    log_i = gr[0:HEADS, :]
    log_f = _log_sigmoid(gr[HEADS:GATE_COLS, :])
    lane_in_chunk = lax.broadcasted_iota(jnp.int32, log_i.shape, 1) % chunk
    b = _chunk_scan(log_f, jnp.add, 0.0, lane_in_chunk, chunk)
    a = log_i - b
    a_max = _chunk_scan(a, jnp.maximum, -jnp.inf, lane_in_chunk, chunk)
    gates_ref[...] = jnp.concatenate([b, a, a_max], axis=0)


def _prenorm(x2, norm_w, w_in_t, layer, bias_col, tm, chunk):
    t = x2.shape[0]
    return pl.pallas_call(
        functools.partial(_prenorm_kernel, chunk=chunk),
        grid=(t // tm,),
        in_specs=[
            pl.BlockSpec((tm, D_MODEL), lambda i: (i, 0)),
            pl.BlockSpec((1, D_MODEL), lambda i: (0, 0)),
            pl.BlockSpec((None, GATE_COLS, D_MODEL), lambda i: (layer, OFF_GATES // GATE_COLS, 0)),
            pl.BlockSpec((GATE_COLS, 1), lambda i: (0, 0)),
        ],
        out_specs=[
            pl.BlockSpec((tm, D_MODEL), lambda i: (i, 0)),
            pl.BlockSpec((3 * HEADS, tm), lambda i: (0, i)),
        ],
        out_shape=[
            jax.ShapeDtypeStruct((t, D_MODEL), BF16),
            jax.ShapeDtypeStruct((3 * HEADS, t), F32),
        ],
        compiler_params=pltpu.CompilerParams(
            dimension_semantics=("parallel",), vmem_limit_bytes=VMEM_LIMIT),
        name="prenorm",
    )(x2, norm_w, w_in_t, bias_col)


def _proj_kernel(h_ref, w_ref, wn_ref, cw_ref, cb_ref, o_ref, wb_ref, stage_ref,
                 *, tm, tn, tiles_per_seq):
    j = pl.program_id(0)
    i = pl.program_id(1)
    seg = D_MODEL // tn
    shift_tile = OFF_GATES // tn
    nrows = 256

    @pl.when(jnp.logical_and(i == 0, j < shift_tile))
    def _():
        for r in range(0, tn, nrows):
            wb_ref[r:r + nrows, :] = w_ref[r:r + nrows, :].astype(BF16)

    @pl.when(jnp.logical_and(i == 0, j >= shift_tile))
    def _():
        for r in range(0, tn - nrows, nrows):
            wb_ref[r:r + nrows, :] = w_ref[r + GATE_COLS:r + GATE_COLS + nrows, :].astype(BF16)
        r = tn - nrows
        wb_ref[r:tn - GATE_COLS, :] = w_ref[r + GATE_COLS:tn, :].astype(BF16)
        wb_ref[tn - GATE_COLS:tn, :] = wn_ref[...].astype(BF16)

    def matmul(c):
        return lax.dot_general(h_ref[...], wb_ref[c:c + MXU_COLS, :], (((1,), (1,)), ((), ())),
                               preferred_element_type=F32)

    def stripes(epilogue):
        for c in range(0, tn, MXU_COLS):
            o_ref[:, c:c + MXU_COLS] = epilogue(matmul(c)).astype(BF16)

    is_conv = j < 2 * seg
    pool_tile = shift_tile
    is_plain = jnp.logical_or(jnp.logical_and(j >= 2 * seg, j < 3 * seg), j == pool_tile)
    is_silu = jnp.logical_or(jnp.logical_and(j >= 4 * seg, j < 5 * seg),
                             jnp.logical_and(j > pool_tile, j < pool_tile + 2 * (POOL_WIDTH // tn)))
    is_sigmoid = jnp.logical_not(jnp.logical_or(jnp.logical_or(is_conv, is_plain), is_silu))

    @pl.when(jnp.logical_and(i == 0, j == 0))
    def _():
        stage_ref[...] = jnp.zeros_like(stage_ref)

    def conv_tile(out_scale):
        seq_start = i % tiles_per_seq == 0
        groups = CONV_ROWS // SUBLANES + 1
        sublane = lax.broadcasted_iota(jnp.int32, (groups, SUBLANES, MXU_COLS), 1)

        def shift_down(a, n):
            rot = pltpu.roll(a, n, 1)
            above = jnp.concatenate([rot[:1], rot[:groups - 1]], axis=0)
            return jnp.where(sublane < n, above, rot)

        never = jnp.full((1, MXU_COLS), i, jnp.int32) < 0

        def stage(s, p):
            c = s * MXU_COLS
            if p == 0:
                history = stage_ref[s, tm:tm + SUBLANES, :]
                stage_ref[s, 0:SUBLANES, :] = jnp.where(seq_start, 0.0, history)
            r = lax.dot_general(h_ref[p * STAGE_ROWS:(p + 1) * STAGE_ROWS, :],
                                wb_ref[c:c + MXU_COLS, :], (((1,), (1,)), ((), ())),
                                preferred_element_type=F32)
            stage_ref[s, SUBLANES + p * STAGE_ROWS:SUBLANES + (p + 1) * STAGE_ROWS, :] = r
            return jnp.where(never, r[STAGE_ROWS - 1:STAGE_ROWS, :], 0.0)

        def finish(s, p, drained):
            c = s * MXU_COLS
            w = [cw_ref[k:k + 1, c:c + MXU_COLS] + drained for k in range(CONV_WIDTH)]
            for r0 in range(p * STAGE_ROWS, (p + 1) * STAGE_ROWS, CONV_ROWS):
                ext = stage_ref[s, r0:r0 + CONV_ROWS + SUBLANES, :]
                ext = ext.reshape(groups, SUBLANES, MXU_COLS)
                ext1 = shift_down(ext, 1)
                acc = (w[3] * ext + w[2] * ext1) + shift_down(w[1] * ext + w[0] * ext1, 2)
                y = acc[1:].reshape(CONV_ROWS, MXU_COLS) + cb_ref[:, c:c + MXU_COLS]
                act = _silu(y.astype(BF16))
                o_ref[r0:r0 + CONV_ROWS, c:c + MXU_COLS] = (
                    act if out_scale is None else act * out_scale)

        pieces = [(s, p) for s in range(0, tn // MXU_COLS, 2) for p in range(tm // STAGE_ROWS)]

        def stage_pair(s, p):
            return stage(s, p) + stage(s + 1, p)

        def finish_pair(s, p, drained):
            finish(s, p, drained)
            finish(s + 1, p, drained)

        tokens = [stage_pair(*pieces[0])]
        for prev, cur in zip(pieces[:-1], pieces[1:]):
            tokens.append(stage_pair(*cur))
            finish_pair(*prev, tokens[-2])
        finish_pair(*pieces[-1], tokens[-1])

    @pl.when(j < seg)
    def _():
        conv_tile(HEAD_DIM ** -0.5)

    @pl.when(jnp.logical_and(j >= seg, j < 2 * seg))
    def _():
        conv_tile(None)

    @pl.when(is_plain)
    def _():
        stripes(lambda r: r)

    @pl.when(is_silu)
    def _():
        stripes(lambda r: _silu(r.astype(BF16)))

    @pl.when(is_sigmoid)
    def _():
        stripes(lambda r: _sigmoid(r.astype(BF16)))


def _proj(h, w_in_t, layer, conv_w, conv_b, seq, tm, tn):
    t = h.shape[0]
    n_tiles = MAIN_COLS // tn
    shift_tile = OFF_GATES // tn
    conv_tiles = 2 * D_MODEL // tn
    gate_blocks_per_tile = tn // GATE_COLS
    conv_block = lambda j, i: (0, jnp.minimum(j, conv_tiles - 1))
    return pl.pallas_call(
        functools.partial(_proj_kernel, tm=tm, tn=tn, tiles_per_seq=seq // tm),
        grid=(n_tiles, t // tm),
        in_specs=[
            pl.BlockSpec((tm, D_MODEL), lambda j, i: (i, 0)),
            pl.BlockSpec((None, tn, D_MODEL), lambda j, i: (layer, j, 0)),
            pl.BlockSpec((None, GATE_COLS, D_MODEL),
                         lambda j, i: (layer, jnp.where(j >= shift_tile, (j + 1) * gate_blocks_per_tile, 0), 0)),
            pl.BlockSpec((CONV_WIDTH, tn), conv_block),
            pl.BlockSpec((1, tn), conv_block),
        ],
        out_specs=pl.BlockSpec((tm, tn), lambda j, i: (i, j)),
        out_shape=jax.ShapeDtypeStruct((t, MAIN_COLS), BF16),
        scratch_shapes=[pltpu.VMEM((tn, D_MODEL), BF16),
                        pltpu.VMEM((tn // MXU_COLS, tm + SUBLANES, MXU_COLS), F32)],
        compiler_params=pltpu.CompilerParams(
            dimension_semantics=("arbitrary", "arbitrary"), vmem_limit_bytes=VMEM_LIMIT),
        name="proj",
    )(h, w_in_t, w_in_t, conv_w, conv_b)


def _mlstm_kernel(q_ref, k_ref, v_ref, o_ref, z_ref, gates_ref, nw_ref, *rest, chunk, n_cast):
    L = chunk
    d = HEAD_DIM
    cast_in, (y_ref, *cast_out), (caug_ref, m_ref) = (
        rest[:n_cast], rest[n_cast:2 * n_cast + 1], rest[2 * n_cast + 1:])

    for src, dst in zip(cast_in, cast_out):
        dst[...] = src[...].astype(BF16)

    @pl.when(pl.program_id(1) == 0)
    def _():
        caug_ref[...] = jnp.zeros_like(caug_ref)
        m_ref[...] = jnp.zeros_like(m_ref)

    b_row = gates_ref[0:HEADS, :]
    a_row = gates_ref[HEADS:2 * HEADS, :]
    a_max = gates_ref[2 * HEADS:3 * HEADS, :]
    m_st = m_ref[...]
    big_m = jnp.maximum(a_max, m_st)
    g = b_row[:, L - 1:L]
    m_new = jnp.maximum(g + m_st, g + a_max[:, L - 1:L])
    decay = jnp.exp(g + m_st - m_new)
    w_inter = jnp.exp(m_st - big_m)
    floor = jnp.exp(-(b_row + big_m))
    w_new = jnp.exp(g + a_row - m_new)
    m_ref[...] = m_new

    rows = jnp.concatenate(
        [big_m, w_inter, floor, w_new, jnp.zeros((LANES - 4 * HEADS, L), F32)], axis=0)
    cols = rows.T

    def column(j):
        return jnp.broadcast_to(cols[:, j:j + 1], (L, LANES))

    def twice(x):
        return jnp.concatenate([x, x], axis=1)

    t_idx = lax.broadcasted_iota(jnp.int32, (L, L), 0)
    s_idx = lax.broadcasted_iota(jnp.int32, (L, L), 1)
    causal = s_idx <= t_idx
    ones_blk = jnp.ones((L, LANES), BF16)
    ones_sq = jnp.ones((d, LANES), BF16)

    def head_stages(hh):
        col = hh * d
        v = {}

        def load_and_score():
            v["qb"] = q_ref[:, col:col + d]
            v["kb"] = k_ref[:, col:col + d]
            v["vaug"] = jnp.concatenate([v_ref[:, col:col + d], ones_blk], axis=1)
            v["caug"] = caug_ref[hh]
            v["s"] = lax.dot_general(v["qb"], v["kb"], (((1,), (1,)), ((), ())),
                                     preferred_element_type=F32)
            v["qc"] = jnp.dot(v["qb"], v["caug"].astype(BF16),
                              preferred_element_type=F32)

        def intra():
            log_d = jnp.where(causal, a_row[hh:hh + 1, :] - twice(column(hh)), -jnp.inf)
            p = jnp.exp(log_d) * v.pop("s")
            v["pv"] = jnp.dot(p.astype(BF16), v["vaug"], preferred_element_type=F32)

        def combine():
            w_inter_c = column(HEADS + hh)
            pv, qc = v.pop("pv"), v.pop("qc")
            num = pv[:, :d] + twice(w_inter_c) * qc[:, :d]
            den = pv[:, d:] + w_inter_c * qc[:, d:]
            v["inv"] = 1.0 / jnp.maximum(jnp.abs(den), column(2 * HEADS + hh))
            v["ssq"] = jnp.dot((num * num).astype(BF16), ones_sq, preferred_element_type=F32)
            v["num"] = num

        def emit():
            inv = v.pop("inv")
            r = inv * lax.rsqrt(v.pop("ssq") * (1.0 / d) * (inv * inv) + RMS_EPS)
            hn = v.pop("num") * twice(r) * nw_ref[:, col:col + d]
            og = o_ref[:, col:col + d].astype(F32)
            za = z_ref[:, col:col + d].astype(F32)
            y_ref[:, col:col + d] = (og * hn * za).astype(BF16)

        def carry():
            wk = (twice(column(3 * HEADS + hh)) * v.pop("kb").astype(F32)).astype(BF16)
            upd = lax.dot_general(wk, v.pop("vaug"), (((0,), (0,)), ((), ())),
                                  preferred_element_type=F32)
            caug_ref[hh] = decay[hh:hh + 1, :] * v.pop("caug") + upd

        return (load_and_score, intra, combine, emit, carry)

    schedule = sorted((hh * HEAD_SKEW + k, hh, k) for hh in range(HEADS) for k in range(5))
    all_stages = [head_stages(hh) for hh in range(HEADS)]
    for _, hh, k in schedule:
        all_stages[hh][k]()


def _mlstm(proj, gates, norm_w, layer, weights, bsz, seq, chunk):
    nc = seq // chunk
    steps = bsz * nc
    t = bsz * seq
    col_block = lambda cb: pl.BlockSpec((chunk, D_MODEL), lambda b, c: (b * nc + c, cb))
    const = lambda shape: pl.BlockSpec(shape, lambda b, c: (0,) * len(shape))
    cast_in, cast_out, cast_shapes = [], [], []
    for w in weights:
        _, rows, cols = w.shape
        assert rows % (steps * 2 * SUBLANES) == 0
        cast_in.append(pl.BlockSpec((None, rows // steps, cols), lambda b, c: (layer, b * nc + c, 0)))
        cast_out.append(pl.BlockSpec((rows // steps, cols), lambda b, c: (b * nc + c, 0)))
        cast_shapes.append(jax.ShapeDtypeStruct((rows, cols), BF16))
    return pl.pallas_call(
        functools.partial(_mlstm_kernel, chunk=chunk, n_cast=len(weights)),
        grid=(bsz, nc),
        in_specs=[
            col_block(0), col_block(1), col_block(2), col_block(3), col_block(4),
            pl.BlockSpec((3 * HEADS, chunk), lambda b, c: (0, b * nc + c)),
            const((1, D_MODEL)),
        ] + cast_in,
        out_specs=[pl.BlockSpec((chunk, D_MODEL), lambda b, c: (b * nc + c, 0))] + cast_out,
        out_shape=[jax.ShapeDtypeStruct((t, D_MODEL), BF16)] + cast_shapes,
        scratch_shapes=[
            pltpu.VMEM((HEADS, HEAD_DIM, HEAD_DIM + LANES), F32),
            pltpu.VMEM((HEADS, 1), F32),
        ],
        compiler_params=pltpu.CompilerParams(
            dimension_semantics=("parallel", "arbitrary"), vmem_limit_bytes=VMEM_LIMIT),
        name="mlstm",
    )(proj, proj, proj, proj, proj, gates, norm_w, *weights)


def _pool_kernel(u_ref, z_ref, pw_ref, ps_ref, y_ref, ext_ref, *, rows):
    c = pl.program_id(1)

    @pl.when(c == 0)
    def _():
        ext_ref[0:HALO_POOL, :] = jnp.zeros((HALO_POOL, POOL_WIDTH), F32)

    ext_ref[HALO_POOL:HALO_POOL + rows, :] = u_ref[...].astype(F32)
    pos = lax.broadcasted_iota(jnp.int32, (rows, 1), 0) + c * rows + 1
    history_groups = HALO_POOL // SUBLANES
    groups = rows // SUBLANES + history_groups
    sublane = lax.broadcasted_iota(jnp.int32, (groups, SUBLANES, POOL_GROUP_DIM), 1)

    def shift_down(a, j):
        if j % SUBLANES == 0:
            k = j // SUBLANES
            return jnp.concatenate([a[groups - k:], a[:groups - k]], axis=0)
        rot = pltpu.roll(a, j, 1)
        above = jnp.concatenate([rot[groups - 1:], rot[:groups - 1]], axis=0)
        return jnp.where(sublane < j, above, rot)

    for g in range(POOL_GROUPS):
        col = g * POOL_GROUP_DIM
        win = POOL_WINDOWS[g]
        ext = ext_ref[:, col:col + POOL_GROUP_DIM].reshape(groups, SUBLANES, POOL_GROUP_DIM)
        wsum, width = ext, 1
        while width < win:
            wsum = wsum + shift_down(wsum, width)
            width *= 2
        u = ext[history_groups:].reshape(rows, POOL_GROUP_DIM)
        wsum = wsum[history_groups:].reshape(rows, POOL_GROUP_DIM)
        count = jnp.minimum(pos, win).astype(F32)
        pooled = wsum / count - u
        mixed = jnp.dot(pooled.astype(BF16), pw_ref[g], preferred_element_type=F32)
        mixed = mixed * ps_ref[:, col:col + POOL_GROUP_DIM]
        zb = z_ref[:, col:col + POOL_GROUP_DIM].astype(F32)
        y_ref[:, col:col + POOL_GROUP_DIM] = (mixed * zb).astype(BF16)

    ext_ref[0:HALO_POOL, :] = ext_ref[rows:rows + HALO_POOL, :]


def _pool(proj, pool_w, pool_scale, bsz, seq, rows):
    nc = seq // rows
    t = bsz * seq
    u_block = OFF_GATES // POOL_WIDTH
    return pl.pallas_call(
        functools.partial(_pool_kernel, rows=rows),
        grid=(bsz, nc),
        in_specs=[
            pl.BlockSpec((rows, POOL_WIDTH), lambda b, c: (b * nc + c, u_block)),
            pl.BlockSpec((rows, POOL_WIDTH), lambda b, c: (b * nc + c, u_block + 1)),
            pl.BlockSpec((POOL_GROUPS, POOL_GROUP_DIM, POOL_GROUP_DIM), lambda b, c: (0, 0, 0)),
            pl.BlockSpec((1, POOL_WIDTH), lambda b, c: (0, 0)),
        ],
        out_specs=pl.BlockSpec((rows, POOL_WIDTH), lambda b, c: (b * nc + c, 0)),
        out_shape=jax.ShapeDtypeStruct((t, POOL_WIDTH), BF16),
        scratch_shapes=[pltpu.VMEM((rows + HALO_POOL, POOL_WIDTH), F32)],
        compiler_params=pltpu.CompilerParams(
            dimension_semantics=("parallel", "arbitrary"), vmem_limit_bytes=VMEM_LIMIT),
        name="pool",
    )(proj, proj, pool_w, pool_scale)


def _merge_kernel(ya_ref, yb_ref, ga_ref, gb_ref, wa_ref, wb_ref, o_ref):
    br_a = jnp.dot(ya_ref[...], wa_ref[...], preferred_element_type=F32)
    br_b = jnp.dot(yb_ref[...], wb_ref[...], preferred_element_type=F32)
    merged = ga_ref[...].astype(F32) * br_a + gb_ref[...].astype(F32) * br_b
    o_ref[...] = merged.astype(BF16)


def _merge(y_a, y_b, proj, w_a, w_b, tm):
    t = y_a.shape[0]
    ga_block = (OFF_GATES + 2 * POOL_WIDTH) // D_MODEL
    return pl.pallas_call(
        _merge_kernel,
        grid=(t // tm,),
        in_specs=[
            pl.BlockSpec((tm, D_MODEL), lambda i: (i, 0)),
            pl.BlockSpec((tm, POOL_WIDTH), lambda i: (i, 0)),
            pl.BlockSpec((tm, D_MODEL), lambda i: (i, ga_block)),
            pl.BlockSpec((tm, D_MODEL), lambda i: (i, ga_block + 1)),
            pl.BlockSpec((D_MODEL, D_MODEL), lambda i: (0, 0)),
            pl.BlockSpec((POOL_WIDTH, D_MODEL), lambda i: (0, 0)),
        ],
        out_specs=pl.BlockSpec((tm, D_MODEL), lambda i: (i, 0)),
        out_shape=jax.ShapeDtypeStruct((t, D_MODEL), BF16),
        compiler_params=pltpu.CompilerParams(
            dimension_semantics=("parallel",), vmem_limit_bytes=VMEM_LIMIT),
        name="merge",
    )(y_a, y_b, proj, proj, w_a, w_b)


def _outproj_kernel(m_ref, w_ref, x_ref, nw_ref, o_ref):
    out = jnp.dot(m_ref[...], w_ref[...], preferred_element_type=F32)
    y = out * lax.rsqrt(jnp.mean(out * out, axis=-1, keepdims=True) + RMS_EPS) * nw_ref[...]
    o_ref[...] = x_ref[...] + y


def _outproj(merged, w_out, x2, norm_w, tm):
    t = merged.shape[0]
    return pl.pallas_call(
        _outproj_kernel,
        grid=(t // tm,),
        in_specs=[
            pl.BlockSpec((tm, D_MODEL), lambda i: (i, 0)),
            pl.BlockSpec((D_MODEL, D_MODEL), lambda i: (0, 0)),
            pl.BlockSpec((tm, D_MODEL), lambda i: (i, 0)),
            pl.BlockSpec((1, D_MODEL), lambda i: (0, 0)),
        ],
        out_specs=pl.BlockSpec((tm, D_MODEL), lambda i: (i, 0)),
        out_shape=jax.ShapeDtypeStruct((t, D_MODEL), F32),
        compiler_params=pltpu.CompilerParams(
            dimension_semantics=("parallel",), vmem_limit_bytes=VMEM_LIMIT),
        name="outproj",
    )(merged, w_out, x2, norm_w)


def _layer(x2, bsz, seq, layer, norm_pre_w, w_in, i_bias, f_bias, conv_w, conv_b, mlstm_norm_w,
           pool_w, pool_scale, w_proj_mlstm, w_proj_pool, w_out, norm_post_w):
    w_in_t = jnp.swapaxes(w_in, 1, 2)
    bias = jnp.concatenate([i_bias, f_bias]).astype(F32)

    chunk = 256
    h, gates = _prenorm(x2, norm_pre_w[None, :], w_in_t, layer, bias[:, None], tm=1024, chunk=chunk)
    proj = _proj(h, w_in_t, layer, conv_w, conv_b[None, :], seq, tm=1024, tn=1024)
    y_a, w_a, w_b, w_o = _mlstm(proj, gates, mlstm_norm_w[None, :], layer,
                                (w_proj_mlstm, w_proj_pool, w_out), bsz, seq, chunk=chunk)
    y_b = _pool(proj, pool_w.astype(BF16), pool_scale[None, :], bsz, seq, rows=512)
    merged = _merge(y_a, y_b, proj, w_a, w_b, tm=512)
    return _outproj(merged, w_o, x2, norm_post_w[None, :], tm=512)


def kernel(x, norm_pre_w, w_in, mlstm_i_bias, mlstm_f_bias, qk_conv_w, qk_conv_b, mlstm_norm_w,
           pool_w, pool_scale, w_proj_mlstm, w_proj_pool, w_out, norm_post_w):
    bsz, seq, d = x.shape
    assert d == D_MODEL and w_in.shape[-1] == MAIN_COLS + GATE_COLS
    x2 = x.reshape(bsz * seq, d)
    for l in range(norm_pre_w.shape[0]):
        x2 = _layer(x2, bsz, seq, l, norm_pre_w[l], w_in, mlstm_i_bias[l], mlstm_f_bias[l],
                    qk_conv_w[l], qk_conv_b[l], mlstm_norm_w[l], pool_w[l], pool_scale[l],
                    w_proj_mlstm, w_proj_pool, w_out, norm_post_w[l])
    return x2.reshape(bsz, seq, d)
```

```python
import functools

import jax
import jax.numpy as jnp
from jax import lax
from jax.experimental import pallas as pl
from jax.experimental.pallas import tpu as pltpu

D_MODEL = 2048
HEADS = 8
HEAD_DIM = D_MODEL // HEADS
CONV_WIDTH = 4
POOL_GROUPS = 4
POOL_WIDTH = D_MODEL // 2
POOL_GROUP_DIM = POOL_WIDTH // POOL_GROUPS
POOL_WINDOWS = (2, 4, 8, 16)
RMS_EPS = 1e-6
GATE_COLS = 2 * HEADS
OFF_GATES = 5 * D_MODEL
OFF_POOL = OFF_GATES + GATE_COLS
MAIN_COLS = 5 * D_MODEL + 2 * POOL_WIDTH + 2 * D_MODEL

SUBLANES = 8
LANES = 128
MXU_COLS = 256
HALO_POOL = 2 * SUBLANES
assert all(w & (w - 1) == 0 and w <= HALO_POOL for w in POOL_WINDOWS)
VMEM_LIMIT = 56 * 1024 * 1024
STAGE_ROWS = 512
CONV_ROWS = 64
SCORE_LEAD = 2
HEAD_SKEW = 1

F32 = jnp.float32
BF16 = jnp.bfloat16


def _sigmoid(x):
    return 0.5 * jnp.tanh(0.5 * x) + 0.5


def _silu(x):
    half = 0.5 * x
    return half * jnp.tanh(half) + half


def _log_sigmoid(x):
    return jnp.minimum(x, 0.0) - jnp.log1p(jnp.exp(-jnp.abs(x)))


def _chunk_scan(x, combine, fill, lane_in_chunk, chunk):
    sh = 1
    while sh < chunk:
        x = combine(x, jnp.where(lane_in_chunk >= sh, pltpu.roll(x, sh, 1), fill))
        sh *= 2
    return x


def _prenorm_kernel(x_ref, nw_ref, wgt_ref, bias_ref, h_ref, gates_ref, *, chunk):
    x = x_ref[...]
    y = x * lax.rsqrt(jnp.mean(x * x, axis=-1, keepdims=True) + RMS_EPS) * nw_ref[...]
    hb = y.astype(BF16)
    h_ref[...] = hb
    gr = lax.dot_general(wgt_ref[...].astype(BF16), hb, (((1,), (1,)), ((), ())),
                         preferred_element_type=F32) + bias_ref[...]
    log_i = gr[0:HEADS, :]
    log_f = _log_sigmoid(gr[HEADS:GATE_COLS, :])
    lane_in_chunk = lax.broadcasted_iota(jnp.int32, log_i.shape, 1) % chunk
    b = _chunk_scan(log_f, jnp.add, 0.0, lane_in_chunk, chunk)
    a = log_i - b
    a_max = _chunk_scan(a, jnp.maximum, -jnp.inf, lane_in_chunk, chunk)
    gates_ref[...] = jnp.concatenate([b, a, a_max], axis=0)


def _prenorm(x2, norm_w, w_in_t, layer, bias_col, tm, chunk):
    t = x2.shape[0]
    return pl.pallas_call(
        functools.partial(_prenorm_kernel, chunk=chunk),
        grid=(t // tm,),
        in_specs=[
            pl.BlockSpec((tm, D_MODEL), lambda i: (i, 0)),
            pl.BlockSpec((1, D_MODEL), lambda i: (0, 0)),
            pl.BlockSpec((None, GATE_COLS, D_MODEL), lambda i: (layer, OFF_GATES // GATE_COLS, 0)),
            pl.BlockSpec((GATE_COLS, 1), lambda i: (0, 0)),
        ],
        out_specs=[
            pl.BlockSpec((tm, D_MODEL), lambda i: (i, 0)),
            pl.BlockSpec((3 * HEADS, tm), lambda i: (0, i)),
        ],
        out_shape=[
            jax.ShapeDtypeStruct((t, D_MODEL), BF16),
            jax.ShapeDtypeStruct((3 * HEADS, t), F32),
        ],
        compiler_params=pltpu.CompilerParams(
            dimension_semantics=("parallel",), vmem_limit_bytes=VMEM_LIMIT),
        name="prenorm",
    )(x2, norm_w, w_in_t, bias_col)


def _proj_kernel(h_ref, w_ref, wn_ref, cw_ref, cb_ref, o_ref, wb_ref, stage_ref,
                 *, tm, tn, tiles_per_seq):
    j = pl.program_id(0)
    i = pl.program_id(1)
    seg = D_MODEL // tn
    shift_tile = OFF_GATES // tn
    nrows = 256

    @pl.when(jnp.logical_and(i == 0, j < shift_tile))
    def _():
        for r in range(0, tn, nrows):
            wb_ref[r:r + nrows, :] = w_ref[r:r + nrows, :].astype(BF16)

    @pl.when(jnp.logical_and(i == 0, j >= shift_tile))
    def _():
        for r in range(0, tn - nrows, nrows):
            wb_ref[r:r + nrows, :] = w_ref[r + GATE_COLS:r + GATE_COLS + nrows, :].astype(BF16)
        r = tn - nrows
        wb_ref[r:tn - GATE_COLS, :] = w_ref[r + GATE_COLS:tn, :].astype(BF16)
        wb_ref[tn - GATE_COLS:tn, :] = wn_ref[...].astype(BF16)

    def matmul(c):
        return lax.dot_general(h_ref[...], wb_ref[c:c + MXU_COLS, :], (((1,), (1,)), ((), ())),
                               preferred_element_type=F32)

    def stripes(epilogue):
        for c in range(0, tn, MXU_COLS):
            o_ref[:, c:c + MXU_COLS] = epilogue(matmul(c)).astype(BF16)

    is_conv = j < 2 * seg
    pool_tile = shift_tile
    is_plain = jnp.logical_or(jnp.logical_and(j >= 2 * seg, j < 3 * seg), j == pool_tile)
    is_silu = jnp.logical_or(jnp.logical_and(j >= 4 * seg, j < 5 * seg),
                             jnp.logical_and(j > pool_tile, j < pool_tile + 2 * (POOL_WIDTH // tn)))
    is_sigmoid = jnp.logical_not(jnp.logical_or(jnp.logical_or(is_conv, is_plain), is_silu))

    @pl.when(jnp.logical_and(i == 0, j == 0))
    def _():
        stage_ref[...] = jnp.zeros_like(stage_ref)

    def conv_tile(out_scale):
        seq_start = i % tiles_per_seq == 0
        groups = CONV_ROWS // SUBLANES + 1
        sublane = lax.broadcasted_iota(jnp.int32, (groups, SUBLANES, MXU_COLS), 1)

        def shift_down(a, n):
            rot = pltpu.roll(a, n, 1)
            above = jnp.concatenate([rot[:1], rot[:groups - 1]], axis=0)
            return jnp.where(sublane < n, above, rot)

        never = jnp.full((1, MXU_COLS), i, jnp.int32) < 0

        def stage(s, p):
            c = s * MXU_COLS
            if p == 0:
                history = stage_ref[s, tm:tm + SUBLANES, :]
                stage_ref[s, 0:SUBLANES, :] = jnp.where(seq_start, 0.0, history)
            r = lax.dot_general(h_ref[p * STAGE_ROWS:(p + 1) * STAGE_ROWS, :],
                                wb_ref[c:c + MXU_COLS, :], (((1,), (1,)), ((), ())),
                                preferred_element_type=F32)
            stage_ref[s, SUBLANES + p * STAGE_ROWS:SUBLANES + (p + 1) * STAGE_ROWS, :] = r
            return jnp.where(never, r[STAGE_ROWS - 1:STAGE_ROWS, :], 0.0)

        def finish(s, p, drained):
            c = s * MXU_COLS
            w = [cw_ref[k:k + 1, c:c + MXU_COLS] + drained for k in range(CONV_WIDTH)]
            for r0 in range(p * STAGE_ROWS, (p + 1) * STAGE_ROWS, CONV_ROWS):
                ext = stage_ref[s, r0:r0 + CONV_ROWS + SUBLANES, :]
                ext = ext.reshape(groups, SUBLANES, MXU_COLS)
                ext1 = shift_down(ext, 1)
                acc = (w[3] * ext + w[2] * ext1) + shift_down(w[1] * ext + w[0] * ext1, 2)
                y = acc[1:].reshape(CONV_ROWS, MXU_COLS) + cb_ref[:, c:c + MXU_COLS]
                act = _silu(y.astype(BF16))
                o_ref[r0:r0 + CONV_ROWS, c:c + MXU_COLS] = (
                    act if out_scale is None else act * out_scale)

        pieces = [(s, p) for s in range(0, tn // MXU_COLS, 2) for p in range(tm // STAGE_ROWS)]

        def stage_pair(s, p):
            return stage(s, p) + stage(s + 1, p)

        def finish_pair(s, p, drained):
            finish(s, p, drained)
            finish(s + 1, p, drained)

        tokens = [stage_pair(*pieces[0])]
        for prev, cur in zip(pieces[:-1], pieces[1:]):
            tokens.append(stage_pair(*cur))
            finish_pair(*prev, tokens[-2])
        finish_pair(*pieces[-1], tokens[-1])

    @pl.when(j < seg)
    def _():
        conv_tile(HEAD_DIM ** -0.5)

    @pl.when(jnp.logical_and(j >= seg, j < 2 * seg))
    def _():
        conv_tile(None)

    @pl.when(is_plain)
    def _():
        stripes(lambda r: r)

    @pl.when(is_silu)
    def _():
        stripes(lambda r: _silu(r.astype(BF16)))

    @pl.when(is_sigmoid)
    def _():
        stripes(lambda r: _sigmoid(r.astype(BF16)))


def _proj(h, w_in_t, layer, conv_w, conv_b, seq, tm, tn):
    t = h.shape[0]
    n_tiles = MAIN_COLS // tn
    shift_tile = OFF_GATES // tn
    conv_tiles = 2 * D_MODEL // tn
    gate_blocks_per_tile = tn // GATE_COLS
    conv_block = lambda j, i: (0, jnp.minimum(j, conv_tiles - 1))
    return pl.pallas_call(
        functools.partial(_proj_kernel, tm=tm, tn=tn, tiles_per_seq=seq // tm),
        grid=(n_tiles, t // tm),
        in_specs=[
            pl.BlockSpec((tm, D_MODEL), lambda j, i: (i, 0)),
            pl.BlockSpec((None, tn, D_MODEL), lambda j, i: (layer, j, 0)),
            pl.BlockSpec((None, GATE_COLS, D_MODEL),
                         lambda j, i: (layer, jnp.where(j >= shift_tile, (j + 1) * gate_blocks_per_tile, 0), 0)),
            pl.BlockSpec((CONV_WIDTH, tn), conv_block),
            pl.BlockSpec((1, tn), conv_block),
        ],
        out_specs=pl.BlockSpec((tm, tn), lambda j, i: (i, j)),
        out_shape=jax.ShapeDtypeStruct((t, MAIN_COLS), BF16),
        scratch_shapes=[pltpu.VMEM((tn, D_MODEL), BF16),
                        pltpu.VMEM((tn // MXU_COLS, tm + SUBLANES, MXU_COLS), F32)],
        compiler_params=pltpu.CompilerParams(
            dimension_semantics=("arbitrary", "arbitrary"), vmem_limit_bytes=VMEM_LIMIT),
        name="proj",
    )(h, w_in_t, w_in_t, conv_w, conv_b)


def _mlstm_kernel(q_ref, k_ref, v_ref, o_ref, z_ref, gates_ref, nw_ref, *rest, chunk, n_cast):
    L = chunk
    d = HEAD_DIM
    cast_in, (y_ref, *cast_out), (caug_ref, m_ref) = (
        rest[:n_cast], rest[n_cast:2 * n_cast + 1], rest[2 * n_cast + 1:])

    @pl.when(pl.program_id(1) == 0)
    def _():
        caug_ref[...] = jnp.zeros_like(caug_ref)
        m_ref[...] = jnp.zeros_like(m_ref)

    b_row = gates_ref[0:HEADS, :]
    a_row = gates_ref[HEADS:2 * HEADS, :]
    a_max = gates_ref[2 * HEADS:3 * HEADS, :]
    m_st = m_ref[...]
    big_m = jnp.maximum(a_max, m_st)
    g = b_row[:, L - 1:L]
    m_new = jnp.maximum(g + m_st, g + a_max[:, L - 1:L])
    decay = jnp.exp(g + m_st - m_new)
    w_inter = jnp.exp(m_st - big_m)
    floor = jnp.exp(-(b_row + big_m))
    w_new = jnp.exp(g + a_row - m_new)
    m_ref[...] = m_new

    rows = jnp.concatenate(
        [big_m, w_inter, floor, w_new, jnp.zeros((LANES - 4 * HEADS, L), F32)], axis=0)
    cols = rows.T

    def column(j):
        return jnp.broadcast_to(cols[:, j:j + 1], (L, LANES))

    def twice(x):
        return jnp.concatenate([x, x], axis=1)

    t_idx = lax.broadcasted_iota(jnp.int32, (L, L), 0)
    s_idx = lax.broadcasted_iota(jnp.int32, (L, L), 1)
    causal = s_idx <= t_idx
    ones_blk = jnp.ones((L, LANES), BF16)
    ones_sq = jnp.ones((d, LANES), BF16)

    def head_stages(hh):
        col = hh * d
        v = {}

        def load_and_score():
            v["qb"] = q_ref[:, col:col + d]
            v["kb"] = k_ref[:, col:col + d]
            v["vaug"] = jnp.concatenate([v_ref[:, col:col + d], ones_blk], axis=1)
            v["caug"] = caug_ref[hh]
            v["s"] = lax.dot_general(v["qb"], v["kb"], (((1,), (1,)), ((), ())),
                                     preferred_element_type=F32)
            v["qc"] = jnp.dot(v["qb"], v["caug"].astype(BF16),
                              preferred_element_type=F32)

        def intra():
            log_d = jnp.where(causal, a_row[hh:hh + 1, :] - twice(column(hh)), -jnp.inf)
            p = jnp.exp(log_d) * v.pop("s")
            v["pv"] = jnp.dot(p.astype(BF16), v["vaug"], preferred_element_type=F32)

        def combine():
            w_inter_c = column(HEADS + hh)
            pv, qc = v.pop("pv"), v.pop("qc")
            num = pv[:, :d] + twice(w_inter_c) * qc[:, :d]
            den = pv[:, d:] + w_inter_c * qc[:, d:]
            v["inv"] = 1.0 / jnp.maximum(jnp.abs(den), column(2 * HEADS + hh))
            v["ssq"] = jnp.dot((num * num).astype(BF16), ones_sq, preferred_element_type=F32)
            v["num"] = num

        def emit():
            inv = v.pop("inv")
            r = inv * lax.rsqrt(v.pop("ssq") * (1.0 / d) * (inv * inv) + RMS_EPS)
            hn = v.pop("num") * twice(r) * nw_ref[:, col:col + d]
            og = o_ref[:, col:col + d].astype(F32)
            za = z_ref[:, col:col + d].astype(F32)
            y_ref[:, col:col + d] = (og * hn * za).astype(BF16)

        def carry():
            wk = (twice(column(3 * HEADS + hh)) * v.pop("kb").astype(F32)).astype(BF16)
            upd = lax.dot_general(wk, v.pop("vaug"), (((0,), (0,)), ((), ())),
                                  preferred_element_type=F32)
            caug_ref[hh] = decay[hh:hh + 1, :] * v.pop("caug") + upd

        return (load_and_score, intra, combine, emit, carry)

    schedule = sorted((hh * HEAD_SKEW + k - (SCORE_LEAD if k == 0 else 0), hh, k)
                      for hh in range(HEADS) for k in range(5))
    all_stages = [head_stages(hh) for hh in range(HEADS)]
    for _, hh, k in schedule:
        all_stages[hh][k]()

    for src, dst in zip(cast_in, cast_out):
        dst[...] = src[...].astype(BF16)


def _mlstm(proj, gates, norm_w, layer, weights, bsz, seq, chunk):
    nc = seq // chunk
    steps = bsz * nc
    t = bsz * seq
    col_block = lambda cb: pl.BlockSpec((chunk, D_MODEL), lambda b, c: (b * nc + c, cb))
    const = lambda shape: pl.BlockSpec(shape, lambda b, c: (0,) * len(shape))
    cast_in, cast_out, cast_shapes = [], [], []
    for w in weights:
        _, rows, cols = w.shape
        assert rows % (steps * 2 * SUBLANES) == 0
        cast_in.append(pl.BlockSpec((None, rows // steps, cols), lambda b, c: (layer, b * nc + c, 0)))
        cast_out.append(pl.BlockSpec((rows // steps, cols), lambda b, c: (b * nc + c, 0)))
        cast_shapes.append(jax.ShapeDtypeStruct((rows, cols), BF16))
    return pl.pallas_call(
        functools.partial(_mlstm_kernel, chunk=chunk, n_cast=len(weights)),
        grid=(bsz, nc),
        in_specs=[
            col_block(0), col_block(1), col_block(2), col_block(3), col_block(4),
            pl.BlockSpec((3 * HEADS, chunk), lambda b, c: (0, b * nc + c)),
            const((1, D_MODEL)),
        ] + cast_in,
        out_specs=[pl.BlockSpec((chunk, D_MODEL), lambda b, c: (b * nc + c, 0))] + cast_out,
        out_shape=[jax.ShapeDtypeStruct((t, D_MODEL), BF16)] + cast_shapes,
        scratch_shapes=[
            pltpu.VMEM((HEADS, HEAD_DIM, HEAD_DIM + LANES), F32),
            pltpu.VMEM((HEADS, 1), F32),
        ],
        compiler_params=pltpu.CompilerParams(
            dimension_semantics=("parallel", "arbitrary"), vmem_limit_bytes=VMEM_LIMIT),
        name="mlstm",
    )(proj, proj, proj, proj, proj, gates, norm_w, *weights)


def _pool_kernel(u_ref, z_ref, pw_ref, ps_ref, y_ref, ext_ref, *, rows):
    c = pl.program_id(1)

    @pl.when(c == 0)
    def _():
        ext_ref[0:HALO_POOL, :] = jnp.zeros((HALO_POOL, POOL_WIDTH), F32)

    ext_ref[HALO_POOL:HALO_POOL + rows, :] = u_ref[...].astype(F32)
    pos = lax.broadcasted_iota(jnp.int32, (rows, 1), 0) + c * rows + 1
    history_groups = HALO_POOL // SUBLANES
    groups = rows // SUBLANES + history_groups
    sublane = lax.broadcasted_iota(jnp.int32, (groups, SUBLANES, POOL_GROUP_DIM), 1)

    def shift_down(a, j):
        if j % SUBLANES == 0:
            k = j // SUBLANES
            return jnp.concatenate([a[groups - k:], a[:groups - k]], axis=0)
        rot = pltpu.roll(a, j, 1)
        above = jnp.concatenate([rot[groups - 1:], rot[:groups - 1]], axis=0)
        return jnp.where(sublane < j, above, rot)

    for g in range(POOL_GROUPS):
        col = g * POOL_GROUP_DIM
        win = POOL_WINDOWS[g]
        ext = ext_ref[:, col:col + POOL_GROUP_DIM].reshape(groups, SUBLANES, POOL_GROUP_DIM)
        wsum, width = ext, 1
        while width < win:
            wsum = wsum + shift_down(wsum, width)
            width *= 2
        u = ext[history_groups:].reshape(rows, POOL_GROUP_DIM)
        wsum = wsum[history_groups:].reshape(rows, POOL_GROUP_DIM)
        count = jnp.minimum(pos, win).astype(F32)
        pooled = wsum / count - u
        mixed = jnp.dot(pooled.astype(BF16), pw_ref[g], preferred_element_type=F32)
        mixed = mixed * ps_ref[:, col:col + POOL_GROUP_DIM]
        zb = z_ref[:, col:col + POOL_GROUP_DIM].astype(F32)
        y_ref[:, col:col + POOL_GROUP_DIM] = (mixed * zb).astype(BF16)

    ext_ref[0:HALO_POOL, :] = ext_ref[rows:rows + HALO_POOL, :]


def _pool(proj, pool_w, pool_scale, bsz, seq, rows):
    nc = seq // rows
    t = bsz * seq
    u_block = OFF_GATES // POOL_WIDTH
    return pl.pallas_call(
        functools.partial(_pool_kernel, rows=rows),
        grid=(bsz, nc),
        in_specs=[
            pl.BlockSpec((rows, POOL_WIDTH), lambda b, c: (b * nc + c, u_block)),
            pl.BlockSpec((rows, POOL_WIDTH), lambda b, c: (b * nc + c, u_block + 1)),
            pl.BlockSpec((POOL_GROUPS, POOL_GROUP_DIM, POOL_GROUP_DIM), lambda b, c: (0, 0, 0)),
            pl.BlockSpec((1, POOL_WIDTH), lambda b, c: (0, 0)),
        ],
        out_specs=pl.BlockSpec((rows, POOL_WIDTH), lambda b, c: (b * nc + c, 0)),
        out_shape=jax.ShapeDtypeStruct((t, POOL_WIDTH), BF16),
        scratch_shapes=[pltpu.VMEM((rows + HALO_POOL, POOL_WIDTH), F32)],
        compiler_params=pltpu.CompilerParams(
            dimension_semantics=("parallel", "arbitrary"), vmem_limit_bytes=VMEM_LIMIT),
        name="pool",
    )(proj, proj, pool_w, pool_scale)


def _merge_kernel(ya_ref, yb_ref, ga_ref, gb_ref, wa_ref, wb_ref, o_ref):
    br_a = jnp.dot(ya_ref[...], wa_ref[...], preferred_element_type=F32)
    br_b = jnp.dot(yb_ref[...], wb_ref[...], preferred_element_type=F32)
    merged = ga_ref[...].astype(F32) * br_a + gb_ref[...].astype(F32) * br_b
    o_ref[...] = merged.astype(BF16)


def _merge(y_a, y_b, proj, w_a, w_b, tm):
    t = y_a.shape[0]
    ga_block = (OFF_GATES + 2 * POOL_WIDTH) // D_MODEL
    return pl.pallas_call(
        _merge_kernel,
        grid=(t // tm,),
        in_specs=[
            pl.BlockSpec((tm, D_MODEL), lambda i: (i, 0)),
            pl.BlockSpec((tm, POOL_WIDTH), lambda i: (i, 0)),
            pl.BlockSpec((tm, D_MODEL), lambda i: (i, ga_block)),
            pl.BlockSpec((tm, D_MODEL), lambda i: (i, ga_block + 1)),
            pl.BlockSpec((D_MODEL, D_MODEL), lambda i: (0, 0)),
            pl.BlockSpec((POOL_WIDTH, D_MODEL), lambda i: (0, 0)),
        ],
        out_specs=pl.BlockSpec((tm, D_MODEL), lambda i: (i, 0)),
        out_shape=jax.ShapeDtypeStruct((t, D_MODEL), BF16),
        compiler_params=pltpu.CompilerParams(
            dimension_semantics=("parallel",), vmem_limit_bytes=VMEM_LIMIT),
        name="merge",
    )(y_a, y_b, proj, proj, w_a, w_b)


def _outproj_kernel(m_ref, w_ref, x_ref, nw_ref, o_ref):
    out = jnp.dot(m_ref[...], w_ref[...], preferred_element_type=F32)
    y = out * lax.rsqrt(jnp.mean(out * out, axis=-1, keepdims=True) + RMS_EPS) * nw_ref[...]
    o_ref[...] = x_ref[...] + y


def _outproj(merged, w_out, x2, norm_w, tm):
    t = merged.shape[0]
    return pl.pallas_call(
        _outproj_kernel,
        grid=(t // tm,),
        in_specs=[
            pl.BlockSpec((tm, D_MODEL), lambda i: (i, 0)),
            pl.BlockSpec((D_MODEL, D_MODEL), lambda i: (0, 0)),
            pl.BlockSpec((tm, D_MODEL), lambda i: (i, 0)),
            pl.BlockSpec((1, D_MODEL), lambda i: (0, 0)),
        ],
        out_specs=pl.BlockSpec((tm, D_MODEL), lambda i: (i, 0)),
        out_shape=jax.ShapeDtypeStruct((t, D_MODEL), F32),
        compiler_params=pltpu.CompilerParams(
            dimension_semantics=("parallel",), vmem_limit_bytes=VMEM_LIMIT),
        name="outproj",
    )(merged, w_out, x2, norm_w)


def _layer(x2, bsz, seq, layer, norm_pre_w, w_in, i_bias, f_bias, conv_w, conv_b, mlstm_norm_w,
           pool_w, pool_scale, w_proj_mlstm, w_proj_pool, w_out, norm_post_w):
    w_in_t = jnp.swapaxes(w_in, 1, 2)
    bias = jnp.concatenate([i_bias, f_bias]).astype(F32)

    chunk = 256
    h, gates = _prenorm(x2, norm_pre_w[None, :], w_in_t, layer, bias[:, None], tm=1024, chunk=chunk)
    proj = _proj(h, w_in_t, layer, conv_w, conv_b[None, :], seq, tm=1024, tn=1024)
    y_a, w_a, w_b, w_o = _mlstm(proj, gates, mlstm_norm_w[None, :], layer,
                                (w_proj_mlstm, w_proj_pool, w_out), bsz, seq, chunk=chunk)
    y_b = _pool(proj, pool_w.astype(BF16), pool_scale[None, :], bsz, seq, rows=1024)
    merged = _merge(y_a, y_b, proj, w_a, w_b, tm=512)
    return _outproj(merged, w_o, x2, norm_post_w[None, :], tm=512)


def kernel(x, norm_pre_w, w_in, mlstm_i_bias, mlstm_f_bias, qk_conv_w, qk_conv_b, mlstm_norm_w,
           pool_w, pool_scale, w_proj_mlstm, w_proj_pool, w_out, norm_post_w):
    bsz, seq, d = x.shape
    assert d == D_MODEL and w_in.shape[-1] == MAIN_COLS + GATE_COLS
    x2 = x.reshape(bsz * seq, d)
    for l in range(norm_pre_w.shape[0]):
        x2 = _layer(x2, bsz, seq, l, norm_pre_w[l], w_in, mlstm_i_bias[l], mlstm_f_bias[l],
                    qk_conv_w[l], qk_conv_b[l], mlstm_norm_w[l], pool_w[l], pool_scale[l],
                    w_proj_mlstm, w_proj_pool, w_out, norm_post_w[l])
    return x2.reshape(bsz, seq, d)
```

```python
import functools

import jax
import jax.numpy as jnp
from jax import lax
from jax.experimental import pallas as pl
from jax.experimental.pallas import tpu as pltpu

D_MODEL = 2048
HEADS = 8
HEAD_DIM = D_MODEL // HEADS
CONV_WIDTH = 4
POOL_GROUPS = 4
POOL_WIDTH = D_MODEL // 2
POOL_GROUP_DIM = POOL_WIDTH // POOL_GROUPS
POOL_WINDOWS = (2, 4, 8, 16)
RMS_EPS = 1e-6
GATE_COLS = 2 * HEADS
OFF_GATES = 5 * D_MODEL
OFF_POOL = OFF_GATES + GATE_COLS
MAIN_COLS = 5 * D_MODEL + 2 * POOL_WIDTH + 2 * D_MODEL

SUBLANES = 8
LANES = 128
MXU_COLS = 256
HALO_POOL = 2 * SUBLANES
assert all(w & (w - 1) == 0 and w <= HALO_POOL for w in POOL_WINDOWS)
VMEM_LIMIT = 56 * 1024 * 1024
STAGE_ROWS = 512
CONV_ROWS = 64
SCORE_LEAD = 2
HEAD_SKEW = 1

F32 = jnp.float32
BF16 = jnp.bfloat16


def _sigmoid(x):
    return 0.5 * jnp.tanh(0.5 * x) + 0.5


def _silu(x):
    half = 0.5 * x
    return half * jnp.tanh(half) + half


def _log_sigmoid(x):
    return jnp.minimum(x, 0.0) - jnp.log1p(jnp.exp(-jnp.abs(x)))


def _chunk_scan(x, combine, fill, lane_in_chunk, chunk):
    sh = 1
    while sh < chunk:
        x = combine(x, jnp.where(lane_in_chunk >= sh, pltpu.roll(x, sh, 1), fill))
        sh *= 2
    return x


def _prenorm_kernel(x_ref, nw_ref, wgt_ref, bias_ref, h_ref, gates_ref, *, chunk):
    x = x_ref[...]
    y = x * lax.rsqrt(jnp.mean(x * x, axis=-1, keepdims=True) + RMS_EPS) * nw_ref[...]
    hb = y.astype(BF16)
    h_ref[...] = hb
    gr = lax.dot_general(wgt_ref[...].astype(BF16), hb, (((1,), (1,)), ((), ())),
                         preferred_element_type=F32) + bias_ref[...]
    log_i = gr[0:HEADS, :]
    log_f = _log_sigmoid(gr[HEADS:GATE_COLS, :])
    lane_in_chunk = lax.broadcasted_iota(jnp.int32, log_i.shape, 1) % chunk
    b = _chunk_scan(log_f, jnp.add, 0.0, lane_in_chunk, chunk)
    a = log_i - b
    a_max = _chunk_scan(a, jnp.maximum, -jnp.inf, lane_in_chunk, chunk)
    gates_ref[...] = jnp.concatenate([b, a, a_max], axis=0)


def _prenorm(x2, norm_w, w_in_t, layer, bias_col, tm, chunk):
    t = x2.shape[0]
    return pl.pallas_call(
        functools.partial(_prenorm_kernel, chunk=chunk),
        grid=(t // tm,),
        in_specs=[
            pl.BlockSpec((tm, D_MODEL), lambda i: (i, 0)),
            pl.BlockSpec((1, D_MODEL), lambda i: (0, 0)),
            pl.BlockSpec((None, GATE_COLS, D_MODEL), lambda i: (layer, OFF_GATES // GATE_COLS, 0)),
            pl.BlockSpec((GATE_COLS, 1), lambda i: (0, 0)),
        ],
        out_specs=[
            pl.BlockSpec((tm, D_MODEL), lambda i: (i, 0)),
            pl.BlockSpec((3 * HEADS, tm), lambda i: (0, i)),
        ],
        out_shape=[
            jax.ShapeDtypeStruct((t, D_MODEL), BF16),
            jax.ShapeDtypeStruct((3 * HEADS, t), F32),
        ],
        compiler_params=pltpu.CompilerParams(
            dimension_semantics=("parallel",), vmem_limit_bytes=VMEM_LIMIT),
        name="prenorm",
    )(x2, norm_w, w_in_t, bias_col)


def _proj_kernel(h_ref, w_ref, wn_ref, cw_ref, cb_ref, o_ref, wb_ref, stage_ref,
                 *, tm, tn, tiles_per_seq):
    j = pl.program_id(0)
    i = pl.program_id(1)
    seg = D_MODEL // tn
    shift_tile = OFF_GATES // tn
    nrows = 256

    @pl.when(jnp.logical_and(i == 0, j < shift_tile))
    def _():
        for r in range(0, tn, nrows):
            wb_ref[r:r + nrows, :] = w_ref[r:r + nrows, :].astype(BF16)

    @pl.when(jnp.logical_and(i == 0, j >= shift_tile))
    def _():
        for r in range(0, tn - nrows, nrows):
            wb_ref[r:r + nrows, :] = w_ref[r + GATE_COLS:r + GATE_COLS + nrows, :].astype(BF16)
        r = tn - nrows
        wb_ref[r:tn - GATE_COLS, :] = w_ref[r + GATE_COLS:tn, :].astype(BF16)
        wb_ref[tn - GATE_COLS:tn, :] = wn_ref[...].astype(BF16)

    def matmul(c):
        return lax.dot_general(h_ref[...], wb_ref[c:c + MXU_COLS, :], (((1,), (1,)), ((), ())),
                               preferred_element_type=F32)

    def stripes(epilogue):
        for c in range(0, tn, MXU_COLS):
            o_ref[:, c:c + MXU_COLS] = epilogue(matmul(c)).astype(BF16)

    is_conv = j < 2 * seg
    pool_tile = shift_tile
    is_plain = jnp.logical_or(jnp.logical_and(j >= 2 * seg, j < 3 * seg), j == pool_tile)
    is_silu = jnp.logical_or(jnp.logical_and(j >= 4 * seg, j < 5 * seg),
                             jnp.logical_and(j > pool_tile, j < pool_tile + 2 * (POOL_WIDTH // tn)))
    is_sigmoid = jnp.logical_not(jnp.logical_or(jnp.logical_or(is_conv, is_plain), is_silu))

    @pl.when(jnp.logical_and(i == 0, j == 0))
    def _():
        stage_ref[...] = jnp.zeros_like(stage_ref)

    def conv_tile(out_scale):
        seq_start = i % tiles_per_seq == 0
        groups = CONV_ROWS // SUBLANES + 1
        sublane = lax.broadcasted_iota(jnp.int32, (groups, SUBLANES, MXU_COLS), 1)

        def shift_down(a, n):
            rot = pltpu.roll(a, n, 1)
            above = jnp.concatenate([rot[:1], rot[:groups - 1]], axis=0)
            return jnp.where(sublane < n, above, rot)

        never = jnp.full((1, MXU_COLS), i, jnp.int32) < 0

        def stage(s, p):
            c = s * MXU_COLS
            if p == 0:
                history = stage_ref[s, tm:tm + SUBLANES, :]
                stage_ref[s, 0:SUBLANES, :] = jnp.where(seq_start, 0.0, history)
            r = lax.dot_general(h_ref[p * STAGE_ROWS:(p + 1) * STAGE_ROWS, :],
                                wb_ref[c:c + MXU_COLS, :], (((1,), (1,)), ((), ())),
                                preferred_element_type=F32)
            stage_ref[s, SUBLANES + p * STAGE_ROWS:SUBLANES + (p + 1) * STAGE_ROWS, :] = r
            return jnp.where(never, r[STAGE_ROWS - 1:STAGE_ROWS, :], 0.0)

        def finish(s, p, drained):
            c = s * MXU_COLS
            w = [cw_ref[k:k + 1, c:c + MXU_COLS] + drained for k in range(CONV_WIDTH)]
            for r0 in range(p * STAGE_ROWS, (p + 1) * STAGE_ROWS, CONV_ROWS):
                ext = stage_ref[s, r0:r0 + CONV_ROWS + SUBLANES, :]
                ext = ext.reshape(groups, SUBLANES, MXU_COLS)
                ext1 = shift_down(ext, 1)
                acc = (w[3] * ext + w[2] * ext1) + shift_down(w[1] * ext + w[0] * ext1, 2)
                y = acc[1:].reshape(CONV_ROWS, MXU_COLS) + cb_ref[:, c:c + MXU_COLS]
                act = _silu(y.astype(BF16))
                o_ref[r0:r0 + CONV_ROWS, c:c + MXU_COLS] = (
                    act if out_scale is None else act * out_scale)

        pieces = [(s, p) for s in range(0, tn // MXU_COLS, 2) for p in range(tm // STAGE_ROWS)]

        def stage_pair(s, p):
            return stage(s, p) + stage(s + 1, p)

        def finish_pair(s, p, drained):
            finish(s, p, drained)
            finish(s + 1, p, drained)

        tokens = [stage_pair(*pieces[0])]
        for prev, cur in zip(pieces[:-1], pieces[1:]):
            tokens.append(stage_pair(*cur))
            finish_pair(*prev, tokens[-2])
        finish_pair(*pieces[-1], tokens[-1])

    @pl.when(j < seg)
    def _():
        conv_tile(HEAD_DIM ** -0.5)

    @pl.when(jnp.logical_and(j >= seg, j < 2 * seg))
    def _():
        conv_tile(None)

    @pl.when(is_plain)
    def _():
        stripes(lambda r: r)

    @pl.when(is_silu)
    def _():
        stripes(lambda r: _silu(r.astype(BF16)))

    @pl.when(is_sigmoid)
    def _():
        stripes(lambda r: _sigmoid(r.astype(BF16)))


def _proj(h, w_in_t, layer, conv_w, conv_b, seq, tm, tn):
    t = h.shape[0]
    n_tiles = MAIN_COLS // tn
    shift_tile = OFF_GATES // tn
    conv_tiles = 2 * D_MODEL // tn
    gate_blocks_per_tile = tn // GATE_COLS
    conv_block = lambda j, i: (0, jnp.minimum(j, conv_tiles - 1))
    return pl.pallas_call(
        functools.partial(_proj_kernel, tm=tm, tn=tn, tiles_per_seq=seq // tm),
        grid=(n_tiles, t // tm),
        in_specs=[
            pl.BlockSpec((tm, D_MODEL), lambda j, i: (i, 0)),
            pl.BlockSpec((None, tn, D_MODEL), lambda j, i: (layer, j, 0)),
            pl.BlockSpec((None, GATE_COLS, D_MODEL),
                         lambda j, i: (layer, jnp.where(j >= shift_tile, (j + 1) * gate_blocks_per_tile, 0), 0)),
            pl.BlockSpec((CONV_WIDTH, tn), conv_block),
            pl.BlockSpec((1, tn), conv_block),
        ],
        out_specs=pl.BlockSpec((tm, tn), lambda j, i: (i, j)),
        out_shape=jax.ShapeDtypeStruct((t, MAIN_COLS), BF16),
        scratch_shapes=[pltpu.VMEM((tn, D_MODEL), BF16),
                        pltpu.VMEM((tn // MXU_COLS, tm + SUBLANES, MXU_COLS), F32)],
        compiler_params=pltpu.CompilerParams(
            dimension_semantics=("arbitrary", "arbitrary"), vmem_limit_bytes=VMEM_LIMIT),
        name="proj",
    )(h, w_in_t, w_in_t, conv_w, conv_b)


def _mlstm_kernel(q_ref, k_ref, v_ref, o_ref, z_ref, gates_ref, nw_ref, *rest, chunk, n_cast):
    L = chunk
    d = HEAD_DIM
    cast_in, (y_ref, *cast_out), (caug_ref, m_ref) = (
        rest[:n_cast], rest[n_cast:2 * n_cast + 1], rest[2 * n_cast + 1:])

    @pl.when(pl.program_id(1) == 0)
    def _():
        caug_ref[...] = jnp.zeros_like(caug_ref)
        m_ref[...] = jnp.zeros_like(m_ref)

    b_row = gates_ref[0:HEADS, :]
    a_row = gates_ref[HEADS:2 * HEADS, :]
    a_max = gates_ref[2 * HEADS:3 * HEADS, :]
    m_st = m_ref[...]
    big_m = jnp.maximum(a_max, m_st)
    g = b_row[:, L - 1:L]
    m_new = jnp.maximum(g + m_st, g + a_max[:, L - 1:L])
    decay = jnp.exp(g + m_st - m_new)
    w_inter = jnp.exp(m_st - big_m)
    floor = jnp.exp(-(b_row + big_m))
    w_new = jnp.exp(g + a_row - m_new)
    m_ref[...] = m_new

    rows = jnp.concatenate(
        [big_m, w_inter, floor, w_new, jnp.zeros((LANES - 4 * HEADS, L), F32)], axis=0)
    cols = rows.T

    def column(j):
        return jnp.broadcast_to(cols[:, j:j + 1], (L, LANES))

    def twice(x):
        return jnp.concatenate([x, x], axis=1)

    t_idx = lax.broadcasted_iota(jnp.int32, (L, L), 0)
    s_idx = lax.broadcasted_iota(jnp.int32, (L, L), 1)
    causal = s_idx <= t_idx
    ones_blk = jnp.ones((L, LANES), BF16)
    ones_sq = jnp.ones((d, LANES), BF16)

    def head_stages(hh):
        col = hh * d
        v = {}

        def load_and_score():
            v["qb"] = q_ref[:, col:col + d]
            v["kb"] = k_ref[:, col:col + d]
            v["vaug"] = jnp.concatenate([v_ref[:, col:col + d], ones_blk], axis=1)
            v["caug"] = caug_ref[hh]
            v["s"] = lax.dot_general(v["qb"], v["kb"], (((1,), (1,)), ((), ())),
                                     preferred_element_type=F32)
            v["qc"] = jnp.dot(v["qb"], v["caug"].astype(BF16),
                              preferred_element_type=F32)

        def intra():
            log_d = jnp.where(causal, a_row[hh:hh + 1, :] - twice(column(hh)), -jnp.inf)
            p = jnp.exp(log_d) * v.pop("s")
            v["pv"] = jnp.dot(p.astype(BF16), v["vaug"], preferred_element_type=F32)

        def combine():
            w_inter_c = column(HEADS + hh)
            pv, qc = v.pop("pv"), v.pop("qc")
            num = pv[:, :d] + twice(w_inter_c) * qc[:, :d]
            den = pv[:, d:] + w_inter_c * qc[:, d:]
            v["inv"] = 1.0 / jnp.maximum(jnp.abs(den), column(2 * HEADS + hh))
            v["ssq"] = jnp.dot((num * num).astype(BF16), ones_sq, preferred_element_type=F32)
            v["num"] = num

        def emit():
            inv = v.pop("inv")
            r = inv * lax.rsqrt(v.pop("ssq") * (1.0 / d) * (inv * inv) + RMS_EPS)
            hn = v.pop("num") * twice(r) * nw_ref[:, col:col + d]
            gate = o_ref[:, col:col + d] * z_ref[:, col:col + d]
            y_ref[:, col:col + d] = (hn * gate.astype(F32)).astype(BF16)

        def carry():
            wk = twice(column(3 * HEADS + hh).astype(BF16)) * v.pop("kb")
            upd = lax.dot_general(wk, v.pop("vaug"), (((0,), (0,)), ((), ())),
                                  preferred_element_type=F32)
            caug_ref[hh] = decay[hh:hh + 1, :] * v.pop("caug") + upd

        return (load_and_score, intra, combine, emit, carry)

    schedule = sorted((hh * HEAD_SKEW + k - (SCORE_LEAD if k == 0 else 0), hh, k)
                      for hh in range(HEADS) for k in range(5))
    all_stages = [head_stages(hh) for hh in range(HEADS)]
    for _, hh, k in schedule:
        all_stages[hh][k]()

    for src, dst in zip(cast_in, cast_out):
        dst[...] = src[...].astype(BF16)


def _mlstm(proj, gates, norm_w, layer, weights, bsz, seq, chunk):
    nc = seq // chunk
    steps = bsz * nc
    t = bsz * seq
    col_block = lambda cb: pl.BlockSpec((chunk, D_MODEL), lambda b, c: (b * nc + c, cb))
    const = lambda shape: pl.BlockSpec(shape, lambda b, c: (0,) * len(shape))
    cast_in, cast_out, cast_shapes = [], [], []
    for w in weights:
        _, rows, cols = w.shape
        assert rows % (steps * 2 * SUBLANES) == 0
        cast_in.append(pl.BlockSpec((None, rows // steps, cols), lambda b, c: (layer, b * nc + c, 0)))
        cast_out.append(pl.BlockSpec((rows // steps, cols), lambda b, c: (b * nc + c, 0)))
        cast_shapes.append(jax.ShapeDtypeStruct((rows, cols), BF16))
    return pl.pallas_call(
        functools.partial(_mlstm_kernel, chunk=chunk, n_cast=len(weights)),
        grid=(bsz, nc),
        in_specs=[
            col_block(0), col_block(1), col_block(2), col_block(3), col_block(4),
            pl.BlockSpec((3 * HEADS, chunk), lambda b, c: (0, b * nc + c)),
            const((1, D_MODEL)),
        ] + cast_in,
        out_specs=[pl.BlockSpec((chunk, D_MODEL), lambda b, c: (b * nc + c, 0))] + cast_out,
        out_shape=[jax.ShapeDtypeStruct((t, D_MODEL), BF16)] + cast_shapes,
        scratch_shapes=[
            pltpu.VMEM((HEADS, HEAD_DIM, HEAD_DIM + LANES), F32),
            pltpu.VMEM((HEADS, 1), F32),
        ],
        compiler_params=pltpu.CompilerParams(
            dimension_semantics=("parallel", "arbitrary"), vmem_limit_bytes=VMEM_LIMIT),
        name="mlstm",
    )(proj, proj, proj, proj, proj, gates, norm_w, *weights)


def _pool_kernel(u_ref, z_ref, pw_ref, ps_ref, y_ref, ext_ref, *, rows):
    c = pl.program_id(1)

    @pl.when(c == 0)
    def _():
        ext_ref[0:HALO_POOL, :] = jnp.zeros((HALO_POOL, POOL_WIDTH), F32)

    ext_ref[HALO_POOL:HALO_POOL + rows, :] = u_ref[...].astype(F32)
    pos = lax.broadcasted_iota(jnp.int32, (rows, 1), 0) + c * rows + 1
    history_groups = HALO_POOL // SUBLANES
    groups = rows // SUBLANES + history_groups
    sublane = lax.broadcasted_iota(jnp.int32, (groups, SUBLANES, POOL_GROUP_DIM), 1)

    def shift_down(a, j):
        if j % SUBLANES == 0:
            k = j // SUBLANES
            return jnp.concatenate([a[groups - k:], a[:groups - k]], axis=0)
        rot = pltpu.roll(a, j, 1)
        above = jnp.concatenate([rot[groups - 1:], rot[:groups - 1]], axis=0)
        return jnp.where(sublane < j, above, rot)

    for g in range(POOL_GROUPS):
        col = g * POOL_GROUP_DIM
        win = POOL_WINDOWS[g]
        ext = ext_ref[:, col:col + POOL_GROUP_DIM].reshape(groups, SUBLANES, POOL_GROUP_DIM)
        wsum, width = ext, 1
        while width < win:
            wsum = wsum + shift_down(wsum, width)
            width *= 2
        u = ext[history_groups:].reshape(rows, POOL_GROUP_DIM)
        wsum = wsum[history_groups:].reshape(rows, POOL_GROUP_DIM)
        count = jnp.minimum(pos, win).astype(F32)
        pooled = wsum / count - u
        mixed = jnp.dot(pooled.astype(BF16), pw_ref[g], preferred_element_type=F32)
        mixed = mixed * ps_ref[:, col:col + POOL_GROUP_DIM]
        zb = z_ref[:, col:col + POOL_GROUP_DIM].astype(F32)
        y_ref[:, col:col + POOL_GROUP_DIM] = (mixed * zb).astype(BF16)

    ext_ref[0:HALO_POOL, :] = ext_ref[rows:rows + HALO_POOL, :]


def _pool(proj, pool_w, pool_scale, bsz, seq, rows):
    nc = seq // rows
    t = bsz * seq
    u_block = OFF_GATES // POOL_WIDTH
    return pl.pallas_call(
        functools.partial(_pool_kernel, rows=rows),
        grid=(bsz, nc),
        in_specs=[
            pl.BlockSpec((rows, POOL_WIDTH), lambda b, c: (b * nc + c, u_block)),
            pl.BlockSpec((rows, POOL_WIDTH), lambda b, c: (b * nc + c, u_block + 1)),
            pl.BlockSpec((POOL_GROUPS, POOL_GROUP_DIM, POOL_GROUP_DIM), lambda b, c: (0, 0, 0)),
            pl.BlockSpec((1, POOL_WIDTH), lambda b, c: (0, 0)),
        ],
        out_specs=pl.BlockSpec((rows, POOL_WIDTH), lambda b, c: (b * nc + c, 0)),
        out_shape=jax.ShapeDtypeStruct((t, POOL_WIDTH), BF16),
        scratch_shapes=[pltpu.VMEM((rows + HALO_POOL, POOL_WIDTH), F32)],
        compiler_params=pltpu.CompilerParams(
            dimension_semantics=("parallel", "arbitrary"), vmem_limit_bytes=VMEM_LIMIT),
        name="pool",
    )(proj, proj, pool_w, pool_scale)


def _merge_kernel(ya_ref, yb_ref, ga_ref, gb_ref, wa_ref, wb_ref, o_ref):
    br_a = jnp.dot(ya_ref[...], wa_ref[...], preferred_element_type=F32)
    br_b = jnp.dot(yb_ref[...], wb_ref[...], preferred_element_type=F32)
    merged = ga_ref[...].astype(F32) * br_a + gb_ref[...].astype(F32) * br_b
    o_ref[...] = merged.astype(BF16)


def _merge(y_a, y_b, proj, w_a, w_b, tm):
    t = y_a.shape[0]
    ga_block = (OFF_GATES + 2 * POOL_WIDTH) // D_MODEL
    return pl.pallas_call(
        _merge_kernel,
        grid=(t // tm,),
        in_specs=[
            pl.BlockSpec((tm, D_MODEL), lambda i: (i, 0)),
            pl.BlockSpec((tm, POOL_WIDTH), lambda i: (i, 0)),
            pl.BlockSpec((tm, D_MODEL), lambda i: (i, ga_block)),
            pl.BlockSpec((tm, D_MODEL), lambda i: (i, ga_block + 1)),
            pl.BlockSpec((D_MODEL, D_MODEL), lambda i: (0, 0)),
            pl.BlockSpec((POOL_WIDTH, D_MODEL), lambda i: (0, 0)),
        ],
        out_specs=pl.BlockSpec((tm, D_MODEL), lambda i: (i, 0)),
        out_shape=jax.ShapeDtypeStruct((t, D_MODEL), BF16),
        compiler_params=pltpu.CompilerParams(
            dimension_semantics=("parallel",), vmem_limit_bytes=VMEM_LIMIT),
        name="merge",
    )(y_a, y_b, proj, proj, w_a, w_b)


def _outproj_kernel(m_ref, w_ref, x_ref, nw_ref, o_ref):
    out = jnp.dot(m_ref[...], w_ref[...], preferred_element_type=F32)
    y = out * lax.rsqrt(jnp.mean(out * out, axis=-1, keepdims=True) + RMS_EPS) * nw_ref[...]
    o_ref[...] = x_ref[...] + y


def _outproj(merged, w_out, x2, norm_w, tm):
    t = merged.shape[0]
    return pl.pallas_call(
        _outproj_kernel,
        grid=(t // tm,),
        in_specs=[
            pl.BlockSpec((tm, D_MODEL), lambda i: (i, 0)),
            pl.BlockSpec((D_MODEL, D_MODEL), lambda i: (0, 0)),
            pl.BlockSpec((tm, D_MODEL), lambda i: (i, 0)),
            pl.BlockSpec((1, D_MODEL), lambda i: (0, 0)),
        ],
        out_specs=pl.BlockSpec((tm, D_MODEL), lambda i: (i, 0)),
        out_shape=jax.ShapeDtypeStruct((t, D_MODEL), F32),
        compiler_params=pltpu.CompilerParams(
            dimension_semantics=("parallel",), vmem_limit_bytes=VMEM_LIMIT),
        name="outproj",
    )(merged, w_out, x2, norm_w)


def _layer(x2, bsz, seq, layer, norm_pre_w, w_in, i_bias, f_bias, conv_w, conv_b, mlstm_norm_w,
           pool_w, pool_scale, w_proj_mlstm, w_proj_pool, w_out, norm_post_w):
    w_in_t = jnp.swapaxes(w_in, 1, 2)
    bias = jnp.concatenate([i_bias, f_bias]).astype(F32)

    chunk = 256
    h, gates = _prenorm(x2, norm_pre_w[None, :], w_in_t, layer, bias[:, None], tm=1024, chunk=chunk)
    proj = _proj(h, w_in_t, layer, conv_w, conv_b[None, :], seq, tm=1024, tn=1024)
    y_a, w_a, w_b, w_o = _mlstm(proj, gates, mlstm_norm_w[None, :], layer,
                                (w_proj_mlstm, w_proj_pool, w_out), bsz, seq, chunk=chunk)
    y_b = _pool(proj, pool_w.astype(BF16), pool_scale[None, :], bsz, seq, rows=1024)
    merged = _merge(y_a, y_b, proj, w_a, w_b, tm=512)
    return _outproj(merged, w_o, x2, norm_post_w[None, :], tm=512)


def kernel(x, norm_pre_w, w_in, mlstm_i_bias, mlstm_f_bias, qk_conv_w, qk_conv_b, mlstm_norm_w,
           pool_w, pool_scale, w_proj_mlstm, w_proj_pool, w_out, norm_post_w):
    bsz, seq, d = x.shape
    assert d == D_MODEL and w_in.shape[-1] == MAIN_COLS + GATE_COLS
    x2 = x.reshape(bsz * seq, d)
    for l in range(norm_pre_w.shape[0]):
        x2 = _layer(x2, bsz, seq, l, norm_pre_w[l], w_in, mlstm_i_bias[l], mlstm_f_bias[l],
                    qk_conv_w[l], qk_conv_b[l], mlstm_norm_w[l], pool_w[l], pool_scale[l],
                    w_proj_mlstm, w_proj_pool, w_out, norm_post_w[l])
    return x2.reshape(bsz, seq, d)
```

```python
import functools

import jax
import jax.numpy as jnp
from jax import lax
from jax.experimental import pallas as pl
from jax.experimental.pallas import tpu as pltpu

D_MODEL = 2048
HEADS = 8
HEAD_DIM = D_MODEL // HEADS
CONV_WIDTH = 4
POOL_GROUPS = 4
POOL_WIDTH = D_MODEL // 2
POOL_GROUP_DIM = POOL_WIDTH // POOL_GROUPS
POOL_WINDOWS = (2, 4, 8, 16)
RMS_EPS = 1e-6
GATE_COLS = 2 * HEADS
OFF_GATES = 5 * D_MODEL
OFF_POOL = OFF_GATES + GATE_COLS
MAIN_COLS = 5 * D_MODEL + 2 * POOL_WIDTH + 2 * D_MODEL

SUBLANES = 8
LANES = 128
MXU_COLS = 256
HALO_POOL = 2 * SUBLANES
assert all(w & (w - 1) == 0 and w <= HALO_POOL for w in POOL_WINDOWS)
VMEM_LIMIT = 56 * 1024 * 1024
STAGE_ROWS = 512
CONV_ROWS = 64
SCORE_LEAD = 2
HEAD_SKEW = 1

F32 = jnp.float32
BF16 = jnp.bfloat16


def _sigmoid(x):
    return 0.5 * jnp.tanh(0.5 * x) + 0.5


def _silu(x):
    half = 0.5 * x
    return half * jnp.tanh(half) + half


def _log_sigmoid(x):
    return jnp.minimum(x, 0.0) - jnp.log1p(jnp.exp(-jnp.abs(x)))


def _chunk_scan(x, combine, fill, lane_in_chunk, chunk):
    sh = 1
    while sh < chunk:
        x = combine(x, jnp.where(lane_in_chunk >= sh, pltpu.roll(x, sh, 1), fill))
        sh *= 2
    return x


def _prenorm_kernel(x_ref, nw_ref, wgt_ref, bias_ref, h_ref, gates_ref, *, chunk):
    x = x_ref[...]
    y = x * lax.rsqrt(jnp.mean(x * x, axis=-1, keepdims=True) + RMS_EPS) * nw_ref[...]
    hb = y.astype(BF16)
    h_ref[...] = hb
    gr = lax.dot_general(wgt_ref[...].astype(BF16), hb, (((1,), (1,)), ((), ())),
                         preferred_element_type=F32) + bias_ref[...]
    log_i = gr[0:HEADS, :]
    log_f = _log_sigmoid(gr[HEADS:GATE_COLS, :])
    lane_in_chunk = lax.broadcasted_iota(jnp.int32, log_i.shape, 1) % chunk
    b = _chunk_scan(log_f, jnp.add, 0.0, lane_in_chunk, chunk)
    a = log_i - b
    a_max = _chunk_scan(a, jnp.maximum, -jnp.inf, lane_in_chunk, chunk)
    gates_ref[...] = jnp.concatenate([b, a, a_max], axis=0)


def _prenorm(x2, norm_w, w_in_t, layer, bias_col, tm, chunk):
    t = x2.shape[0]
    return pl.pallas_call(
        functools.partial(_prenorm_kernel, chunk=chunk),
        grid=(t // tm,),
        in_specs=[
            pl.BlockSpec((tm, D_MODEL), lambda i: (i, 0)),
            pl.BlockSpec((1, D_MODEL), lambda i: (0, 0)),
            pl.BlockSpec((None, GATE_COLS, D_MODEL), lambda i: (layer, OFF_GATES // GATE_COLS, 0)),
            pl.BlockSpec((GATE_COLS, 1), lambda i: (0, 0)),
        ],
        out_specs=[
            pl.BlockSpec((tm, D_MODEL), lambda i: (i, 0)),
            pl.BlockSpec((3 * HEADS, tm), lambda i: (0, i)),
        ],
        out_shape=[
            jax.ShapeDtypeStruct((t, D_MODEL), BF16),
            jax.ShapeDtypeStruct((3 * HEADS, t), F32),
        ],
        compiler_params=pltpu.CompilerParams(
            dimension_semantics=("parallel",), vmem_limit_bytes=VMEM_LIMIT),
        name="prenorm",
    )(x2, norm_w, w_in_t, bias_col)


def _proj_kernel(h_ref, w_ref, wn_ref, cw_ref, cb_ref, o_ref, wb_ref, stage_ref,
                 *, tm, tn, tiles_per_seq):
    j = pl.program_id(0)
    i = pl.program_id(1)
    seg = D_MODEL // tn
    shift_tile = OFF_GATES // tn
    nrows = 256

    @pl.when(jnp.logical_and(i == 0, j < shift_tile))
    def _():
        for r in range(0, tn, nrows):
            wb_ref[r:r + nrows, :] = w_ref[r:r + nrows, :].astype(BF16)

    @pl.when(jnp.logical_and(i == 0, j >= shift_tile))
    def _():
        for r in range(0, tn - nrows, nrows):
            wb_ref[r:r + nrows, :] = w_ref[r + GATE_COLS:r + GATE_COLS + nrows, :].astype(BF16)
        r = tn - nrows
        wb_ref[r:tn - GATE_COLS, :] = w_ref[r + GATE_COLS:tn, :].astype(BF16)
        wb_ref[tn - GATE_COLS:tn, :] = wn_ref[...].astype(BF16)

    def matmul(c):
        return lax.dot_general(h_ref[...], wb_ref[c:c + MXU_COLS, :], (((1,), (1,)), ((), ())),
                               preferred_element_type=F32)

    def stripes(epilogue):
        for c in range(0, tn, MXU_COLS):
            o_ref[:, c:c + MXU_COLS] = epilogue(matmul(c)).astype(BF16)

    is_conv = j < 2 * seg
    pool_tile = shift_tile
    is_plain = jnp.logical_or(jnp.logical_and(j >= 2 * seg, j < 3 * seg), j == pool_tile)
    is_silu = jnp.logical_or(jnp.logical_and(j >= 4 * seg, j < 5 * seg),
                             jnp.logical_and(j > pool_tile, j < pool_tile + 2 * (POOL_WIDTH // tn)))
    is_sigmoid = jnp.logical_not(jnp.logical_or(jnp.logical_or(is_conv, is_plain), is_silu))

    @pl.when(jnp.logical_and(i == 0, j == 0))
    def _():
        stage_ref[...] = jnp.zeros_like(stage_ref)

    def conv_tile(out_scale):
        seq_start = i % tiles_per_seq == 0
        groups = CONV_ROWS // SUBLANES + 1
        sublane = lax.broadcasted_iota(jnp.int32, (groups, SUBLANES, MXU_COLS), 1)

        def shift_down(a, n):
            rot = pltpu.roll(a, n, 1)
            above = jnp.concatenate([rot[:1], rot[:groups - 1]], axis=0)
            return jnp.where(sublane < n, above, rot)

        never = jnp.full((1, MXU_COLS), i, jnp.int32) < 0

        def stage(s, p):
            c = s * MXU_COLS
            if p == 0:
                history = stage_ref[s, tm:tm + SUBLANES, :]
                stage_ref[s, 0:SUBLANES, :] = jnp.where(seq_start, 0.0, history)
            r = lax.dot_general(h_ref[p * STAGE_ROWS:(p + 1) * STAGE_ROWS, :],
                                wb_ref[c:c + MXU_COLS, :], (((1,), (1,)), ((), ())),
                                preferred_element_type=F32)
            stage_ref[s, SUBLANES + p * STAGE_ROWS:SUBLANES + (p + 1) * STAGE_ROWS, :] = r
            return jnp.where(never, r[STAGE_ROWS - 1:STAGE_ROWS, :], 0.0)

        def finish(s, p, drained):
            c = s * MXU_COLS
            w = [cw_ref[k:k + 1, c:c + MXU_COLS] + drained for k in range(CONV_WIDTH)]
            for r0 in range(p * STAGE_ROWS, (p + 1) * STAGE_ROWS, CONV_ROWS):
                ext = stage_ref[s, r0:r0 + CONV_ROWS + SUBLANES, :]
                ext = ext.reshape(groups, SUBLANES, MXU_COLS)
                ext1 = shift_down(ext, 1)
                acc = (w[3] * ext + w[2] * ext1) + shift_down(w[1] * ext + w[0] * ext1, 2)
                y = acc[1:].reshape(CONV_ROWS, MXU_COLS) + cb_ref[:, c:c + MXU_COLS]
                act = _silu(y.astype(BF16))
                o_ref[r0:r0 + CONV_ROWS, c:c + MXU_COLS] = (
                    act if out_scale is None else act * out_scale)

        pieces = [(s, p) for s in range(0, tn // MXU_COLS, 2) for p in range(tm // STAGE_ROWS)]

        def stage_pair(s, p):
            return stage(s, p) + stage(s + 1, p)

        def finish_pair(s, p, drained):
            finish(s, p, drained)
            finish(s + 1, p, drained)

        tokens = [stage_pair(*pieces[0])]
        for prev, cur in zip(pieces[:-1], pieces[1:]):
            tokens.append(stage_pair(*cur))
            finish_pair(*prev, tokens[-2])
        finish_pair(*pieces[-1], tokens[-1])

    @pl.when(j < seg)
    def _():
        conv_tile(HEAD_DIM ** -0.5)

    @pl.when(jnp.logical_and(j >= seg, j < 2 * seg))
    def _():
        conv_tile(None)

    @pl.when(is_plain)
    def _():
        stripes(lambda r: r)

    @pl.when(is_silu)
    def _():
        stripes(lambda r: _silu(r.astype(BF16)))

    @pl.when(is_sigmoid)
    def _():
        stripes(lambda r: _sigmoid(r.astype(BF16)))


def _proj(h, w_in_t, layer, conv_w, conv_b, seq, tm, tn):
    t = h.shape[0]
    n_tiles = MAIN_COLS // tn
    shift_tile = OFF_GATES // tn
    conv_tiles = 2 * D_MODEL // tn
    gate_blocks_per_tile = tn // GATE_COLS
    conv_block = lambda j, i: (0, jnp.minimum(j, conv_tiles - 1))
    return pl.pallas_call(
        functools.partial(_proj_kernel, tm=tm, tn=tn, tiles_per_seq=seq // tm),
        grid=(n_tiles, t // tm),
        in_specs=[
            pl.BlockSpec((tm, D_MODEL), lambda j, i: (i, 0)),
            pl.BlockSpec((None, tn, D_MODEL), lambda j, i: (layer, j, 0)),
            pl.BlockSpec((None, GATE_COLS, D_MODEL),
                         lambda j, i: (layer, jnp.where(j >= shift_tile, (j + 1) * gate_blocks_per_tile, 0), 0)),
            pl.BlockSpec((CONV_WIDTH, tn), conv_block),
            pl.BlockSpec((1, tn), conv_block),
        ],
        out_specs=pl.BlockSpec((tm, tn), lambda j, i: (i, j)),
        out_shape=jax.ShapeDtypeStruct((t, MAIN_COLS), BF16),
        scratch_shapes=[pltpu.VMEM((tn, D_MODEL), BF16),
                        pltpu.VMEM((tn // MXU_COLS, tm + SUBLANES, MXU_COLS), F32)],
        compiler_params=pltpu.CompilerParams(
            dimension_semantics=("arbitrary", "arbitrary"), vmem_limit_bytes=VMEM_LIMIT),
        name="proj",
    )(h, w_in_t, w_in_t, conv_w, conv_b)


def _mlstm_kernel(q_ref, k_ref, v_ref, o_ref, z_ref, gates_ref, nw_ref, *rest, chunk, n_cast):
    L = chunk
    d = HEAD_DIM
    cast_in, (y_ref, *cast_out), (caug_ref, m_ref) = (
        rest[:n_cast], rest[n_cast:2 * n_cast + 1], rest[2 * n_cast + 1:])

    @pl.when(pl.program_id(1) == 0)
    def _():
        caug_ref[...] = jnp.zeros_like(caug_ref)
        m_ref[...] = jnp.zeros_like(m_ref)

    b_row = gates_ref[0:HEADS, :]
    a_row = gates_ref[HEADS:2 * HEADS, :]
    a_max = gates_ref[2 * HEADS:3 * HEADS, :]
    m_st = m_ref[...]
    big_m = jnp.maximum(a_max, m_st)
    g = b_row[:, L - 1:L]
    m_new = jnp.maximum(g + m_st, g + a_max[:, L - 1:L])
    decay = jnp.exp(g + m_st - m_new)
    w_inter = jnp.exp(m_st - big_m)
    floor = jnp.exp(-(b_row + big_m))
    w_new = jnp.exp(g + a_row - m_new)
    m_ref[...] = m_new

    rows = jnp.concatenate(
        [big_m, w_inter, floor, w_new, jnp.zeros((LANES - 4 * HEADS, L), F32)], axis=0)
    cols = rows.T

    def column(j):
        return jnp.broadcast_to(cols[:, j:j + 1], (L, LANES))

    def twice(x):
        return jnp.concatenate([x, x], axis=1)

    t_idx = lax.broadcasted_iota(jnp.int32, (L, L), 0)
    s_idx = lax.broadcasted_iota(jnp.int32, (L, L), 1)
    causal = s_idx <= t_idx
    ones_blk = jnp.ones((L, LANES), BF16)
    ones_sq = jnp.ones((d, LANES), BF16)

    def head_stages(hh):
        col = hh * d
        v = {}

        def load_and_score():
            v["qb"] = q_ref[:, col:col + d]
            v["kb"] = k_ref[:, col:col + d]
            v["vaug"] = jnp.concatenate([v_ref[:, col:col + d], ones_blk], axis=1)
            v["caug"] = caug_ref[hh]
            v["s"] = lax.dot_general(v["qb"], v["kb"], (((1,), (1,)), ((), ())),
                                     preferred_element_type=F32)
            v["qc"] = jnp.dot(v["qb"], v["caug"].astype(BF16),
                              preferred_element_type=F32)

        def intra():
            log_d = jnp.where(causal, a_row[hh:hh + 1, :] - twice(column(hh)), -jnp.inf)
            p = jnp.exp(log_d) * v.pop("s")
            v["pv"] = jnp.dot(p.astype(BF16), v["vaug"], preferred_element_type=F32)

        def combine():
            w_inter_c = column(HEADS + hh)
            pv, qc = v.pop("pv"), v.pop("qc")
            num = pv[:, :d] + twice(w_inter_c) * qc[:, :d]
            den = pv[:, d:] + w_inter_c * qc[:, d:]
            v["inv"] = 1.0 / jnp.maximum(jnp.abs(den), column(2 * HEADS + hh))
            v["ssq"] = jnp.dot((num * num).astype(BF16), ones_sq, preferred_element_type=F32)
            v["num"] = num

        def emit():
            inv = v.pop("inv")
            r = inv * lax.rsqrt(v.pop("ssq") * (1.0 / d) * (inv * inv) + RMS_EPS)
            hn = v.pop("num") * twice(r) * nw_ref[:, col:col + d]
            gate = o_ref[:, col:col + d] * z_ref[:, col:col + d]
            y_ref[:, col:col + d] = (hn * gate.astype(F32)).astype(BF16)

        def carry():
            wk = twice(column(3 * HEADS + hh).astype(BF16)) * v.pop("kb")
            upd = lax.dot_general(wk, v.pop("vaug"), (((0,), (0,)), ((), ())),
                                  preferred_element_type=F32)
            caug_ref[hh] = decay[hh:hh + 1, :] * v.pop("caug") + upd

        return (load_and_score, intra, combine, emit, carry)

    schedule = sorted((hh * HEAD_SKEW + k - (SCORE_LEAD if k == 0 else 0), hh, k)
                      for hh in range(HEADS) for k in range(5))
    all_stages = [head_stages(hh) for hh in range(HEADS)]
    for _, hh, k in schedule:
        all_stages[hh][k]()

    for src, dst in zip(cast_in, cast_out):
        dst[...] = src[...].astype(BF16)


def _mlstm(proj, gates, norm_w, layer, weights, bsz, seq, chunk):
    nc = seq // chunk
    steps = bsz * nc
    t = bsz * seq
    col_block = lambda cb: pl.BlockSpec((chunk, D_MODEL), lambda b, c: (b * nc + c, cb))
    const = lambda shape: pl.BlockSpec(shape, lambda b, c: (0,) * len(shape))
    cast_in, cast_out, cast_shapes = [], [], []
    for w in weights:
        _, rows, cols = w.shape
        assert rows % (steps * 2 * SUBLANES) == 0
        cast_in.append(pl.BlockSpec((None, rows // steps, cols), lambda b, c: (layer, b * nc + c, 0)))
        cast_out.append(pl.BlockSpec((rows // steps, cols), lambda b, c: (b * nc + c, 0)))
        cast_shapes.append(jax.ShapeDtypeStruct((rows, cols), BF16))
    return pl.pallas_call(
        functools.partial(_mlstm_kernel, chunk=chunk, n_cast=len(weights)),
        grid=(bsz, nc),
        in_specs=[
            col_block(0), col_block(1), col_block(2), col_block(3), col_block(4),
            pl.BlockSpec((3 * HEADS, chunk), lambda b, c: (0, b * nc + c)),
            const((1, D_MODEL)),
        ] + cast_in,
        out_specs=[pl.BlockSpec((chunk, D_MODEL), lambda b, c: (b * nc + c, 0))] + cast_out,
        out_shape=[jax.ShapeDtypeStruct((t, D_MODEL), BF16)] + cast_shapes,
        scratch_shapes=[
            pltpu.VMEM((HEADS, HEAD_DIM, HEAD_DIM + LANES), F32),
            pltpu.VMEM((HEADS, 1), F32),
        ],
        compiler_params=pltpu.CompilerParams(
            dimension_semantics=("parallel", "arbitrary"), vmem_limit_bytes=VMEM_LIMIT),
        name="mlstm",
    )(proj, proj, proj, proj, proj, gates, norm_w, *weights)


def _pool_kernel(u_ref, z_ref, pw_ref, ps_ref, y_ref, ext_ref, *, rows):
    c = pl.program_id(1)

    @pl.when(c == 0)
    def _():
        ext_ref[0:HALO_POOL, :] = jnp.zeros((HALO_POOL, POOL_WIDTH), F32)

    ext_ref[HALO_POOL:HALO_POOL + rows, :] = u_ref[...].astype(F32)
    pos = lax.broadcasted_iota(jnp.int32, (rows, 1), 0) + c * rows + 1
    history_groups = HALO_POOL // SUBLANES
    groups = rows // SUBLANES + history_groups
    sublane = lax.broadcasted_iota(jnp.int32, (groups, SUBLANES, POOL_GROUP_DIM), 1)

    def shift_down(a, j):
        if j % SUBLANES == 0:
            k = j // SUBLANES
            return jnp.concatenate([a[groups - k:], a[:groups - k]], axis=0)
        rot = pltpu.roll(a, j, 1)
        above = jnp.concatenate([rot[groups - 1:], rot[:groups - 1]], axis=0)
        return jnp.where(sublane < j, above, rot)

    for g in range(POOL_GROUPS):
        col = g * POOL_GROUP_DIM
        win = POOL_WINDOWS[g]
        ext = ext_ref[:, col:col + POOL_GROUP_DIM].reshape(groups, SUBLANES, POOL_GROUP_DIM)
        wsum, width = ext, 1
        while width < win:
            wsum = wsum + shift_down(wsum, width)
            width *= 2
        u = ext[history_groups:].reshape(rows, POOL_GROUP_DIM)
        wsum = wsum[history_groups:].reshape(rows, POOL_GROUP_DIM)
        count = jnp.minimum(pos, win).astype(F32)
        pooled = wsum / count - u
        mixed = jnp.dot(pooled.astype(BF16), pw_ref[g], preferred_element_type=F32)
        mixed = mixed * ps_ref[:, col:col + POOL_GROUP_DIM]
        zb = z_ref[:, col:col + POOL_GROUP_DIM].astype(F32)
        y_ref[:, col:col + POOL_GROUP_DIM] = (mixed * zb).astype(BF16)

    ext_ref[0:HALO_POOL, :] = ext_ref[rows:rows + HALO_POOL, :]


def _pool(proj, pool_w, pool_scale, bsz, seq, rows):
    nc = seq // rows
    t = bsz * seq
    u_block = OFF_GATES // POOL_WIDTH
    return pl.pallas_call(
        functools.partial(_pool_kernel, rows=rows),
        grid=(bsz, nc),
        in_specs=[
            pl.BlockSpec((rows, POOL_WIDTH), lambda b, c: (b * nc + c, u_block)),
            pl.BlockSpec((rows, POOL_WIDTH), lambda b, c: (b * nc + c, u_block + 1)),
            pl.BlockSpec((POOL_GROUPS, POOL_GROUP_DIM, POOL_GROUP_DIM), lambda b, c: (0, 0, 0)),
            pl.BlockSpec((1, POOL_WIDTH), lambda b, c: (0, 0)),
        ],
        out_specs=pl.BlockSpec((rows, POOL_WIDTH), lambda b, c: (b * nc + c, 0)),
        out_shape=jax.ShapeDtypeStruct((t, POOL_WIDTH), BF16),
        scratch_shapes=[pltpu.VMEM((rows + HALO_POOL, POOL_WIDTH), F32)],
        compiler_params=pltpu.CompilerParams(
            dimension_semantics=("parallel", "arbitrary"), vmem_limit_bytes=VMEM_LIMIT),
        name="pool",
    )(proj, proj, pool_w, pool_scale)


def _merge_kernel(ya_ref, yb_ref, ga_ref, gb_ref, wa_ref, wb_ref, o_ref):
    br_a = jnp.dot(ya_ref[...], wa_ref[...], preferred_element_type=F32)
    br_b = jnp.dot(yb_ref[...], wb_ref[...], preferred_element_type=F32)
    merged = ga_ref[...].astype(F32) * br_a + gb_ref[...].astype(F32) * br_b
    o_ref[...] = merged.astype(BF16)


def _merge(y_a, y_b, proj, w_a, w_b, tm):
    t = y_a.shape[0]
    ga_block = (OFF_GATES + 2 * POOL_WIDTH) // D_MODEL
    return pl.pallas_call(
        _merge_kernel,
        grid=(t // tm,),
        in_specs=[
            pl.BlockSpec((tm, D_MODEL), lambda i: (i, 0)),
            pl.BlockSpec((tm, POOL_WIDTH), lambda i: (i, 0)),
            pl.BlockSpec((tm, D_MODEL), lambda i: (i, ga_block)),
            pl.BlockSpec((tm, D_MODEL), lambda i: (i, ga_block + 1)),
            pl.BlockSpec((D_MODEL, D_MODEL), lambda i: (0, 0)),
            pl.BlockSpec((POOL_WIDTH, D_MODEL), lambda i: (0, 0)),
        ],
        out_specs=pl.BlockSpec((tm, D_MODEL), lambda i: (i, 0)),
        out_shape=jax.ShapeDtypeStruct((t, D_MODEL), BF16),
        compiler_params=pltpu.CompilerParams(
            dimension_semantics=("parallel",), vmem_limit_bytes=VMEM_LIMIT),
        name="merge",
    )(y_a, y_b, proj, proj, w_a, w_b)


def _outproj_kernel(m_ref, w_ref, x_ref, nw_ref, o_ref):
    out = jnp.dot(m_ref[...], w_ref[...], preferred_element_type=F32)
    y = out * lax.rsqrt(jnp.mean(out * out, axis=-1, keepdims=True) + RMS_EPS) * nw_ref[...]
    o_ref[...] = x_ref[...] + y


def _outproj(merged, w_out, x2, norm_w, tm):
    t = merged.shape[0]
    return pl.pallas_call(
        _outproj_kernel,
        grid=(t // tm,),
        in_specs=[
            pl.BlockSpec((tm, D_MODEL), lambda i: (i, 0)),
            pl.BlockSpec((D_MODEL, D_MODEL), lambda i: (0, 0)),
            pl.BlockSpec((tm, D_MODEL), lambda i: (i, 0)),
            pl.BlockSpec((1, D_MODEL), lambda i: (0, 0)),
        ],
        out_specs=pl.BlockSpec((tm, D_MODEL), lambda i: (i, 0)),
        out_shape=jax.ShapeDtypeStruct((t, D_MODEL), F32),
        compiler_params=pltpu.CompilerParams(
            dimension_semantics=("parallel",), vmem_limit_bytes=VMEM_LIMIT),
        name="outproj",
    )(merged, w_out, x2, norm_w)


def _merge_out_kernel(ya_ref, yb_ref, ga_ref, gb_ref, wa_ref, wb_ref, wo_ref, x_ref, nw_ref, o_ref):
    br_a = jnp.dot(ya_ref[...], wa_ref[...], preferred_element_type=F32)
    br_b = jnp.dot(yb_ref[...], wb_ref[...], preferred_element_type=F32)
    merged = ga_ref[...].astype(F32) * br_a + gb_ref[...].astype(F32) * br_b
    out = jnp.dot(merged.astype(BF16), wo_ref[...], preferred_element_type=F32)
    y = out * lax.rsqrt(jnp.mean(out * out, axis=-1, keepdims=True) + RMS_EPS) * nw_ref[...]
    o_ref[...] = x_ref[...] + y


def _merge_out(y_a, y_b, proj, w_a, w_b, w_o, x2, norm_w, tm):
    t = y_a.shape[0]
    ga_block = (OFF_GATES + 2 * POOL_WIDTH) // D_MODEL
    resident = lambda shape: pl.BlockSpec(shape, lambda i: (0, 0), pipeline_mode=pl.Buffered(1))
    return pl.pallas_call(
        _merge_out_kernel,
        grid=(t // tm,),
        in_specs=[
            pl.BlockSpec((tm, D_MODEL), lambda i: (i, 0)),
            pl.BlockSpec((tm, POOL_WIDTH), lambda i: (i, 0)),
            pl.BlockSpec((tm, D_MODEL), lambda i: (i, ga_block)),
            pl.BlockSpec((tm, D_MODEL), lambda i: (i, ga_block + 1)),
            resident((D_MODEL, D_MODEL)), resident((POOL_WIDTH, D_MODEL)), resident((D_MODEL, D_MODEL)),
            pl.BlockSpec((tm, D_MODEL), lambda i: (i, 0)),
            pl.BlockSpec((1, D_MODEL), lambda i: (0, 0)),
        ],
        out_specs=pl.BlockSpec((tm, D_MODEL), lambda i: (i, 0)),
        out_shape=jax.ShapeDtypeStruct((t, D_MODEL), F32),
        compiler_params=pltpu.CompilerParams(
            dimension_semantics=("parallel",), vmem_limit_bytes=VMEM_LIMIT),
        name="merge_out",
    )(y_a, y_b, proj, proj, w_a, w_b, w_o, x2, norm_w)


def _layer(x2, bsz, seq, layer, norm_pre_w, w_in, i_bias, f_bias, conv_w, conv_b, mlstm_norm_w,
           pool_w, pool_scale, w_proj_mlstm, w_proj_pool, w_out, norm_post_w):
    w_in_t = jnp.swapaxes(w_in, 1, 2)
    bias = jnp.concatenate([i_bias, f_bias]).astype(F32)

    chunk = 256
    h, gates = _prenorm(x2, norm_pre_w[None, :], w_in_t, layer, bias[:, None], tm=1024, chunk=chunk)
    proj = _proj(h, w_in_t, layer, conv_w, conv_b[None, :], seq, tm=1024, tn=1024)
    y_a, w_a, w_b, w_o = _mlstm(proj, gates, mlstm_norm_w[None, :], layer,
                                (w_proj_mlstm, w_proj_pool, w_out), bsz, seq, chunk=chunk)
    y_b = _pool(proj, pool_w.astype(BF16), pool_scale[None, :], bsz, seq, rows=1024)
    return _merge_out(y_a, y_b, proj, w_a, w_b, w_o, x2, norm_post_w[None, :], tm=512)


def kernel(x, norm_pre_w, w_in, mlstm_i_bias, mlstm_f_bias, qk_conv_w, qk_conv_b, mlstm_norm_w,
           pool_w, pool_scale, w_proj_mlstm, w_proj_pool, w_out, norm_post_w):
    bsz, seq, d = x.shape
    assert d == D_MODEL and w_in.shape[-1] == MAIN_COLS + GATE_COLS
    x2 = x.reshape(bsz * seq, d)
    for l in range(norm_pre_w.shape[0]):
        x2 = _layer(x2, bsz, seq, l, norm_pre_w[l], w_in, mlstm_i_bias[l], mlstm_f_bias[l],
                    qk_conv_w[l], qk_conv_b[l], mlstm_norm_w[l], pool_w[l], pool_scale[l],
                    w_proj_mlstm, w_proj_pool, w_out, norm_post_w[l])
    return x2.reshape(bsz, seq, d)
```
